```python
import math
import jax, jax.numpy as jnp
from jax import lax
import numpy as np

D_MODEL = 1024
BATCH = 16
SEQ = 2048
DEPTH = 4
DEC_BATCH = 128
DEC_SEQ = 4
PAST_LEN = 8192
PAGE_SIZE = 128

N_AB = (DEPTH + 1) // 2
N_C = DEPTH // 2
MLA_HEADS = 8
Q_LORA = 384
KV_LORA = 256
QK_NOPE = 64
QK_ROPE = 32
V_DIM = 64
ROPE_THETA = 10000.0
Q_BLOCK = 128
ATTN_SCALE = 1.0 / math.sqrt(QK_NOPE + QK_ROPE)
LRU_WIDTH = 512
LRU_HEADS = 8
LRU_HEAD_DIM = LRU_WIDTH // LRU_HEADS
CONV_WIDTH = 4
LRU_C = 8.0
MIX_AB = MLA_HEADS * V_DIM + LRU_WIDTH
IN_AB = Q_LORA + KV_LORA + QK_ROPE + 2 * LRU_WIDTH
SPLIT_AB = [Q_LORA, Q_LORA + KV_LORA, Q_LORA + KV_LORA + QK_ROPE, Q_LORA + KV_LORA + QK_ROPE + LRU_WIDTH]
CHUNK = 128
GMLP_WIDTH = 1024
GMLP_GROUPS = 8
GMLP_GROUP_DIM = GMLP_WIDTH // GMLP_GROUPS
FFN_HIDDEN = -(-8 * D_MODEL // (3 * 256)) * 256
PLE_DIM = 256
EPS = 1e-6

kernel_name = 'hybrid_mla_rglru_gmlp_step'


def rmsnorm(x, g):
    xf = x.astype(jnp.float32)
    y = xf * lax.rsqrt(jnp.mean(xf * xf, axis=-1, keepdims=True) + EPS)
    return (y * g.astype(jnp.float32)).astype(x.dtype)


def layernorm(x, g, b):
    xf = x.astype(jnp.float32)
    xc = xf - jnp.mean(xf, axis=-1, keepdims=True)
    y = xc * lax.rsqrt(jnp.mean(xc * xc, axis=-1, keepdims=True) + EPS)
    return (y * g.astype(jnp.float32) + b.astype(jnp.float32)).astype(x.dtype)


def rope(x, pos):
    half = x.shape[-1] // 2
    inv = jnp.exp(-math.log(ROPE_THETA) * jnp.arange(half, dtype=jnp.float32) / half)
    ang = pos.astype(jnp.float32)[:, None] * inv[None, :]
    shp = (pos.shape[0],) + (1,) * (x.ndim - 3) + (half,)
    cos = jnp.cos(ang).reshape(shp)
    sin = jnp.sin(ang).reshape(shp)
    xf = x.astype(jnp.float32)
    x1, x2 = xf[..., :half], xf[..., half:]
    return jnp.concatenate([x1 * cos - x2 * sin, x2 * cos + x1 * sin], axis=-1).astype(x.dtype)


def mla_project(zq, zkv, zpe, pos, g_q, g_kv, w_uq, w_uk):
    b, t, _ = zq.shape
    q = (rmsnorm(zq, g_q) @ w_uq).reshape(b, t, MLA_HEADS, QK_NOPE + QK_ROPE)
    q_pe = rope(q[..., QK_NOPE:], pos)
    q_lat = jnp.einsum('bthn,hnc->bthc', q[..., :QK_NOPE], w_uk)
    c_kv = rmsnorm(zkv, g_kv)
    k_pe = rope(zpe, pos)
    return q_lat, q_pe, c_kv, k_pe


def mla_prompt_attend(q_lat, q_pe, c_kv, k_pe):
    b, s, h, c = q_lat.shape
    nb = s // Q_BLOCK
    ql = q_lat.reshape(b, nb, Q_BLOCK, h, c).transpose(1, 0, 2, 3, 4)
    qp = q_pe.reshape(b, nb, Q_BLOCK, h, QK_ROPE).transpose(1, 0, 2, 3, 4)
    key_pos = jnp.arange(s)

    def block(args):
        ql_b, qp_b, bi = args
        sc = (jnp.einsum('bqhc,bkc->bhqk', ql_b, c_kv) + jnp.einsum('bqhr,bkr->bhqk', qp_b, k_pe)).astype(jnp.float32) * ATTN_SCALE
        q_pos = bi * Q_BLOCK + jnp.arange(Q_BLOCK)
        sc = jnp.where(key_pos[None, :] <= q_pos[:, None], sc, -jnp.inf)
        pr = jax.nn.softmax(sc, axis=-1).astype(c_kv.dtype)
        return jnp.einsum('bhqk,bkc->bqhc', pr, c_kv)

    o = lax.map(block, (ql, qp, jnp.arange(nb)))
    return o.transpose(1, 0, 2, 3, 4).reshape(b, s, h, c)


def mla_sample_attend(q_lat, q_pe, c_new, pe_new, past_c, past_pe):
    t = q_lat.shape[1]
    n_past = past_c.shape[1]
    s_past = (jnp.einsum('bthc,blc->bhtl', q_lat, past_c) + jnp.einsum('bthr,blr->bhtl', q_pe, past_pe)).astype(jnp.float32) * ATTN_SCALE
    s_new = (jnp.einsum('bthc,bsc->bhts', q_lat, c_new) + jnp.einsum('bthr,bsr->bhts', q_pe, pe_new)).astype(jnp.float32) * ATTN_SCALE
    causal = jnp.tril(jnp.ones((t, t), dtype=bool))
    s_new = jnp.where(causal, s_new, -jnp.inf)
    pr = jax.nn.softmax(jnp.concatenate([s_past, s_new], axis=-1), axis=-1).astype(c_new.dtype)
    return jnp.einsum('bhtl,blc->bthc', pr[..., :n_past], past_c) + jnp.einsum('bhts,bsc->bthc', pr[..., n_past:], c_new)


def rglru_block(zx, zg, conv_buf, h0, conv_w, conv_b, w_rg, b_rg, w_ig, b_ig, lam):
    b, t, w = zx.shape
    xpad = jnp.concatenate([conv_buf.astype(zx.dtype), zx], axis=1)
    xc = conv_b + xpad[:, 0:t] * conv_w[0]
    for k in range(1, CONV_WIDTH):
        xc = xc + xpad[:, k:k + t] * conv_w[k]
    new_buf = xpad[:, t:]
    xh = xc.reshape(b, t, LRU_HEADS, LRU_HEAD_DIM)
    r = jax.nn.sigmoid(jnp.einsum('bthi,hij->bthj', xh, w_rg).reshape(b, t, w) + b_rg).astype(jnp.float32)
    ig = jax.nn.sigmoid(jnp.einsum('bthi,hij->bthj', xh, w_ig).reshape(b, t, w) + b_ig).astype(jnp.float32)
    log_a = -LRU_C * r * jax.nn.softplus(-lam.astype(jnp.float32))
    a = jnp.exp(log_a)
    bx = jnp.sqrt(-jnp.expm1(2.0 * log_a)) * ig * xc.astype(jnp.float32)

    def step(h, ab):
        a_t, b_t = ab
        h = a_t * h + b_t
        return h, h

    h_last, hs = lax.scan(step, h0.astype(jnp.float32), (a.transpose(1, 0, 2), bx.transpose(1, 0, 2)))
    y = hs.transpose(1, 0, 2).astype(zx.dtype) * jax.nn.gelu(zg)
    return y, h_last.astype(h0.dtype), new_buf


def chunk_gmlp(xn, w_in, ln_g, ln_b, w_s, b_s, w_out):
    b, t, _ = xn.shape
    z = jax.nn.gelu(xn @ w_in)
    u, v = z[..., :GMLP_WIDTH], z[..., GMLP_WIDTH:]
    v = layernorm(v, ln_g, ln_b)
    L = min(t, CHUNK)
    nch = t // L
    wm = (w_s[:, :L, :L] * jnp.tril(jnp.ones((L, L), w_s.dtype))).astype(v.dtype)
    vg = v.reshape(b, nch, L, GMLP_GROUPS, GMLP_GROUP_DIM)
    sp = jnp.einsum('gts,bcsgd->bctgd', wm, vg) + b_s[:, :L].T[:, :, None].astype(v.dtype)
    y = (u * sp.reshape(b, t, GMLP_WIDTH)) @ w_out
    return y, v


def trunk(x, p, pos, prm, past):
    b, t, _ = x.shape
    h = x
    ckv, kpe, lru, conv, vrows = [], [], [], [], []
    for i in range(DEPTH):
        j = i // 2
        xn = rmsnorm(h, prm['g_mix'][i])
        if i % 2 == 0:
            z = xn @ prm['w_in_ab'][j]
            zq, zkv, zpe, zx, zg = jnp.split(z, SPLIT_AB, axis=-1)
            q_lat, q_pe, c_kv, k_pe = mla_project(zq, zkv, zpe, pos, prm['g_qnorm'][j], prm['g_kvnorm'][j], prm['w_uq'][j], prm['w_uk'][j])
            if past is None:
                o_lat = mla_prompt_attend(q_lat, q_pe, c_kv, k_pe)
                buf0 = jnp.zeros((b, CONV_WIDTH - 1, LRU_WIDTH), x.dtype)
                h0 = jnp.zeros((b, LRU_WIDTH), x.dtype)
            else:
                pt = past['page_table']
                past_c = past['cache_ckv'][j, pt].reshape(b, -1, KV_LORA).astype(c_kv.dtype)
                past_pe = past['cache_kpe'][j, pt].reshape(b, -1, QK_ROPE).astype(k_pe.dtype)
                o_lat = mla_sample_attend(q_lat, q_pe, c_kv, k_pe, past_c, past_pe)
                buf0 = past['state_conv'][j]
                h0 = past['state_lru'][j]
            attn = jnp.einsum('bthc,hcv->bthv', o_lat, prm['w_uv'][j]).reshape(b, t, MLA_HEADS * V_DIM)
            lru_y, h_last, buf = rglru_block(zx, zg, buf0, h0, prm['conv_w'][j], prm['conv_b'][j], prm['w_rg'][j], prm['b_rg'][j], prm['w_ig'][j], prm['b_ig'][j], prm['lru_lambda'][j])
            mix = jnp.concatenate([attn, lru_y], axis=-1) @ prm['w_out_ab'][j]
            ckv.append(c_kv)
            kpe.append(k_pe)
            lru.append(h_last)
            conv.append(buf)
        else:
            mix, v = chunk_gmlp(xn, prm['w_in_c'][j], prm['ln_g_c'][j], prm['ln_b_c'][j], prm['w_s'][j], prm['b_s'][j], prm['w_out_c'][j])
            if past is not None:
                vrows.append(v)
        h = h + mix
        hn = rmsnorm(h, prm['g_ffn'][i])
        h = h + (jax.nn.silu(hn @ prm['w_gate'][i]) * (hn @ prm['w_up'][i])) @ prm['w_down'][i]
        gate = jax.nn.sigmoid(rmsnorm(h, prm['g_pe'][i]) @ prm['w_pg'][i])
        h = h + (p[i] @ prm['w_pe'][i]) * gate
    return rmsnorm(h, prm['g_final']), ckv, kpe, lru, conv, vrows


def setup_inputs(seed: int = 0) -> dict:
    key = jax.random.key(seed)
    ks = iter(jax.random.split(key, 48))
    f32 = jnp.float32
    n_pages = PAST_LEN // PAGE_SIZE
    n_used = DEC_BATCH * n_pages
    n_phys = n_used + n_used // 4

    def nrm(shape, scale=1.0):
        return jax.random.normal(next(ks), shape, f32) * scale

    def gain(shape):
        return 1.0 + nrm(shape, 0.05)

    x_prompt = nrm((BATCH, SEQ, D_MODEL))
    x_sample = nrm((DEC_BATCH, DEC_SEQ, D_MODEL))
    cache_ckv = nrm((N_AB, n_phys, PAGE_SIZE, KV_LORA))
    cache_kpe = nrm((N_AB, n_phys, PAGE_SIZE, QK_ROPE))
    state_lru = nrm((N_AB, DEC_BATCH, LRU_WIDTH))
    state_conv = nrm((N_AB, DEC_BATCH, CONV_WIDTH - 1, LRU_WIDTH))
    perm = jax.random.permutation(next(ks), n_phys)
    page_table = perm[:n_used].reshape(DEC_BATCH, n_pages).astype(jnp.int32)
    p_prompt = nrm((DEPTH, BATCH, SEQ, PLE_DIM))
    p_sample = nrm((DEPTH, DEC_BATCH, DEC_SEQ, PLE_DIM))
    a0 = jax.random.uniform(next(ks), (N_AB, LRU_WIDTH), f32, 0.9, 0.999)
    s0 = a0 ** (1.0 / LRU_C)
    lru_lambda = jnp.log(s0) - jnp.log1p(-s0)
    return {
        'x_prompt': x_prompt,
        'x_sample': x_sample,
        'cache_ckv': cache_ckv,
        'cache_kpe': cache_kpe,
        'state_lru': state_lru,
        'state_conv': state_conv,
        'page_table': page_table,
        'p_prompt': p_prompt,
        'p_sample': p_sample,
        'g_mix': gain((DEPTH, D_MODEL)),
        'g_ffn': gain((DEPTH, D_MODEL)),
        'g_pe': gain((DEPTH, D_MODEL)),
        'g_final': gain((D_MODEL,)),
        'w_in_ab': nrm((N_AB, D_MODEL, IN_AB), D_MODEL ** -0.5),
        'g_qnorm': gain((N_AB, Q_LORA)),
        'g_kvnorm': gain((N_AB, KV_LORA)),
        'w_uq': nrm((N_AB, Q_LORA, MLA_HEADS * (QK_NOPE + QK_ROPE)), Q_LORA ** -0.5),
        'w_uk': nrm((N_AB, MLA_HEADS, QK_NOPE, KV_LORA), KV_LORA ** -0.5),
        'w_uv': nrm((N_AB, MLA_HEADS, KV_LORA, V_DIM), KV_LORA ** -0.5),
        'conv_w': nrm((N_AB, CONV_WIDTH, LRU_WIDTH), CONV_WIDTH ** -0.5),
        'conv_b': nrm((N_AB, LRU_WIDTH), 0.01),
        'w_rg': nrm((N_AB, LRU_HEADS, LRU_HEAD_DIM, LRU_HEAD_DIM), LRU_HEAD_DIM ** -0.5),
        'b_rg': nrm((N_AB, LRU_WIDTH), 0.01),
        'w_ig': nrm((N_AB, LRU_HEADS, LRU_HEAD_DIM, LRU_HEAD_DIM), LRU_HEAD_DIM ** -0.5),
        'b_ig': nrm((N_AB, LRU_WIDTH), 0.01),
        'lru_lambda': lru_lambda,
        'w_out_ab': nrm((N_AB, MIX_AB, D_MODEL), MIX_AB ** -0.5),
        'w_in_c': nrm((N_C, D_MODEL, 2 * GMLP_WIDTH), D_MODEL ** -0.5),
        'ln_g_c': gain((N_C, GMLP_WIDTH)),
        'ln_b_c': nrm((N_C, GMLP_WIDTH), 0.01),
        'w_s': nrm((N_C, GMLP_GROUPS, CHUNK, CHUNK), CHUNK ** -0.5),
        'b_s': gain((N_C, GMLP_GROUPS, CHUNK)),
        'w_out_c': nrm((N_C, GMLP_WIDTH, D_MODEL), GMLP_WIDTH ** -0.5),
        'w_gate': nrm((DEPTH, D_MODEL, FFN_HIDDEN), D_MODEL ** -0.5),
        'w_up': nrm((DEPTH, D_MODEL, FFN_HIDDEN), D_MODEL ** -0.5),
        'w_down': nrm((DEPTH, FFN_HIDDEN, D_MODEL), FFN_HIDDEN ** -0.5),
        'w_pe': nrm((DEPTH, PLE_DIM, D_MODEL), PLE_DIM ** -0.5),
        'w_pg': nrm((DEPTH, D_MODEL, D_MODEL), D_MODEL ** -0.5),
    }


def reference(x_prompt, x_sample, cache_ckv, cache_kpe, state_lru, state_conv, page_table, p_prompt, p_sample,
              g_mix, g_ffn, g_pe, g_final, w_in_ab, g_qnorm, g_kvnorm, w_uq, w_uk, w_uv, conv_w, conv_b,
              w_rg, b_rg, w_ig, b_ig, lru_lambda, w_out_ab, w_in_c, ln_g_c, ln_b_c, w_s, b_s, w_out_c,
              w_gate, w_up, w_down, w_pe, w_pg):
    prm = {
        'g_mix': g_mix, 'g_ffn': g_ffn, 'g_pe': g_pe, 'g_final': g_final,
        'w_in_ab': w_in_ab, 'g_qnorm': g_qnorm, 'g_kvnorm': g_kvnorm, 'w_uq': w_uq, 'w_uk': w_uk, 'w_uv': w_uv,
        'conv_w': conv_w, 'conv_b': conv_b, 'w_rg': w_rg, 'b_rg': b_rg, 'w_ig': w_ig, 'b_ig': b_ig,
        'lru_lambda': lru_lambda, 'w_out_ab': w_out_ab,
        'w_in_c': w_in_c, 'ln_g_c': ln_g_c, 'ln_b_c': ln_b_c, 'w_s': w_s, 'b_s': b_s, 'w_out_c': w_out_c,
        'w_gate': w_gate, 'w_up': w_up, 'w_down': w_down, 'w_pe': w_pe, 'w_pg': w_pg,
    }
    past = {'cache_ckv': cache_ckv, 'cache_kpe': cache_kpe, 'state_lru': state_lru,
            'state_conv': state_conv, 'page_table': page_table}
    pos_prompt = jnp.arange(x_prompt.shape[1], dtype=jnp.int32)
    past_len = page_table.shape[1] * PAGE_SIZE
    pos_sample = past_len + jnp.arange(x_sample.shape[1], dtype=jnp.int32)
    y_prompt, ckv_p, kpe_p, lru_p, conv_p, _ = trunk(x_prompt, p_prompt, pos_prompt, prm, None)
    y_sample, ckv_s, kpe_s, lru_s, conv_s, v_s = trunk(x_sample, p_sample, pos_sample, prm, past)
    return (y_prompt, y_sample,
            jnp.stack(ckv_p), jnp.stack(kpe_p), jnp.stack(lru_p), jnp.stack(conv_p),
            jnp.stack(ckv_s), jnp.stack(kpe_s), jnp.stack(lru_s), jnp.stack(conv_s),
            jnp.stack(v_s))
```

```python
import functools
import math

import jax
import jax.numpy as jnp
from jax import lax
from jax.experimental import pallas as pl
from jax.experimental.pallas import tpu as pltpu

F32 = jnp.float32
BF16 = jnp.bfloat16

D_MODEL = 1024
DEPTH = 4
PAGE_SIZE = 128
MLA_HEADS = 8
Q_LORA = 384
KV_LORA = 256
QK_NOPE = 64
QK_ROPE = 32
ROPE_HALF = QK_ROPE // 2
V_DIM = 64
ROPE_THETA = 10000.0
ATTN_SCALE = 1.0 / math.sqrt(QK_NOPE + QK_ROPE)
LRU_WIDTH = 512
LRU_HEADS = 8
LRU_HEAD_DIM = LRU_WIDTH // LRU_HEADS
CONV_WIDTH = 4
LRU_C = 8.0
CHUNK = 128
GMLP_WIDTH = 1024
GMLP_GROUPS = 8
GMLP_GROUP_DIM = GMLP_WIDTH // GMLP_GROUPS
FFN_HIDDEN = -(-8 * D_MODEL // (3 * 256)) * 256
PLE_DIM = 256
EPS = 1e-6

LANES = 128
SUBLANES = 8
QK_CAT = KV_LORA + LANES
IN_AB_PAD = Q_LORA + KV_LORA + 2 * LRU_WIDTH + LANES
VMEM_LIMIT = 56 * 1024 * 1024

ROW_TILE = 256
ATTN_TILE = 256
LRU_TILE = 256
PAGES_PER_CHUNK = 16


def _const_spec(shape):
    nd = len(shape)
    return pl.BlockSpec(shape, lambda *_: (0,) * nd, pipeline_mode=pl.Buffered(1))


def _layer_spec(shape, layer):
    nd = len(shape)
    return pl.BlockSpec((None,) + tuple(shape), lambda *_: (layer,) + (0,) * nd,
                        pipeline_mode=pl.Buffered(1))


def _rms(x, g):
    return x * lax.rsqrt(jnp.mean(x * x, axis=-1, keepdims=True) + EPS) * g


def _dot(a, b):
    return jnp.dot(a, b, preferred_element_type=F32)


def _dot_nt(a, b):
    return lax.dot_general(a, b, (((1,), (1,)), ((), ())), preferred_element_type=F32)


def _rope_block(blk, cs, sn):
    lane = lax.broadcasted_iota(jnp.int32, blk.shape, 1)
    swapped = jnp.where(lane < ROPE_HALF,
                        pltpu.roll(blk, LANES - ROPE_HALF, 1),
                        pltpu.roll(blk, ROPE_HALF, 1))
    return blk * cs + swapped * sn


def _in_even_kernel(h_ref, cs_ref, sn_ref, gmix_ref, win_ref, gq_ref, wuq_ref, wuk_ref, gkv_ref,
                    qcat_ref, kcat_ref, ckv_ref, kpe_ref, zx_ref, zg_ref):
    xn = _rms(h_ref[...], gmix_ref[...]).astype(BF16)
    z = _dot(xn, win_ref[...])
    o_kv = Q_LORA
    o_x = o_kv + KV_LORA
    o_g = o_x + LRU_WIDTH
    o_pe = o_g + LRU_WIDTH
    zx_ref[...] = z[:, o_x:o_g]
    zg_ref[...] = z[:, o_g:o_pe]
    cs = cs_ref[...]
    sn = sn_ref[...]
    ckv = _rms(z[:, o_kv:o_x], gkv_ref[...])
    kpe = _rope_block(z[:, o_pe:o_pe + LANES], cs, sn)
    ckv_ref[...] = ckv
    kpe_ref[...] = kpe[:, :QK_ROPE]
    kcat_ref[:, :KV_LORA] = ckv.astype(BF16)
    kcat_ref[:, KV_LORA:] = kpe.astype(BF16)
    qn = _rms(z[:, :Q_LORA], gq_ref[...]).astype(BF16)
    q = _dot(qn, wuq_ref[...])
    o_rope = MLA_HEADS * QK_NOPE
    for hh in range(MLA_HEADS):
        q_nope = q[:, hh * QK_NOPE:(hh + 1) * QK_NOPE].astype(BF16)
        qcat_ref[hh, :, :KV_LORA] = _dot(q_nope, wuk_ref[hh]).astype(BF16)
        q_pe = _rope_block(q[:, o_rope + hh * LANES:o_rope + (hh + 1) * LANES], cs, sn)
        qcat_ref[hh, :, KV_LORA:] = q_pe.astype(BF16)


def _in_even(h, cs, sn, prm, j, n_pos_tiles):
    n = h.shape[0]
    tm = ROW_TILE
    grid = (n // tm,)
    row = lambda w: pl.BlockSpec((tm, w), lambda r: (r, 0))
    pos = pl.BlockSpec((tm, LANES), lambda r: (r % n_pos_tiles, 0))
    out_shape = (
        jax.ShapeDtypeStruct((MLA_HEADS, n, QK_CAT), BF16),
        jax.ShapeDtypeStruct((n, QK_CAT), BF16),
        jax.ShapeDtypeStruct((n, KV_LORA), F32),
        jax.ShapeDtypeStruct((n, QK_ROPE), F32),
        jax.ShapeDtypeStruct((n, LRU_WIDTH), F32),
        jax.ShapeDtypeStruct((n, LRU_WIDTH), F32),
    )
    return pl.pallas_call(
        _in_even_kernel,
        grid=grid,
        in_specs=[
            row(D_MODEL), pos, pos,
            _layer_spec((1, D_MODEL), 2 * j),
            _layer_spec((D_MODEL, IN_AB_PAD), j),
            _layer_spec((1, Q_LORA), j),
            _layer_spec((Q_LORA, MLA_HEADS * (QK_NOPE + LANES)), j),
            _layer_spec((MLA_HEADS, QK_NOPE, KV_LORA), j),
            _layer_spec((1, KV_LORA), j),
        ],
        out_specs=(
            pl.BlockSpec((MLA_HEADS, tm, QK_CAT), lambda r: (0, r, 0)),
            row(QK_CAT), row(KV_LORA), row(QK_ROPE), row(LRU_WIDTH), row(LRU_WIDTH),
        ),
        out_shape=out_shape,
        compiler_params=pltpu.CompilerParams(dimension_semantics=("parallel",),
                                             vmem_limit_bytes=VMEM_LIMIT),
        name="in_even",
    )(h, cs, sn, prm["g_mix"], prm["w_in_ab"], prm["g_qnorm"], prm["w_uq"], prm["w_uk"],
      prm["g_kvnorm"])


def _lru_gates(xc, wgate_ref, brg_ref, big_ref, lam_ref):
    g = _dot(xc.astype(BF16), wgate_ref[...])
    r = jax.nn.sigmoid(g[:, :LRU_WIDTH] + brg_ref[...])
    ig = jax.nn.sigmoid(g[:, LRU_WIDTH:] + big_ref[...])
    x = -lam_ref[...]
    softplus = jnp.maximum(x, 0.0) + jnp.log1p(jnp.exp(-jnp.abs(x)))
    log_a = -LRU_C * r * softplus
    a = jnp.exp(log_a)
    th = jnp.tanh(log_a)
    bx = jnp.sqrt(-2.0 * th / (1.0 - th)) * ig * xc
    return a, bx


def _shift_rows(x, s, fill):
    return jnp.concatenate([jnp.full((s, x.shape[1]), fill, x.dtype), x[:-s]], axis=0)


def _lru_prompt_kernel(zx_ref, zg_ref, cw_ref, cb_ref, wgate_ref, brg_ref, big_ref, lam_ref,
                       y_ref, hlast_ref, conv_ref, xbuf, hprev):
    t = pl.program_id(1)
    tl = zx_ref.shape[0]
    pad = SUBLANES

    @pl.when(t == 0)
    def _():
        xbuf[:pad, :] = jnp.zeros((pad, LRU_WIDTH), F32)
        hprev[...] = jnp.zeros_like(hprev)

    x = zx_ref[...]
    xbuf[pad:, :] = x
    xc = cb_ref[...] + xbuf[pad - 3:pad - 3 + tl, :] * cw_ref[0:1, :]
    xc = xc + xbuf[pad - 2:pad - 2 + tl, :] * cw_ref[1:2, :]
    xc = xc + xbuf[pad - 1:pad - 1 + tl, :] * cw_ref[2:3, :]
    xc = xc + x * cw_ref[3:4, :]
    xbuf[:pad, :] = x[tl - pad:, :]

    a, b = _lru_gates(xc, wgate_ref, brg_ref, big_ref, lam_ref)
    s = 1
    while s < tl:
        b = a * _shift_rows(b, s, 0.0) + b
        a = a * _shift_rows(a, s, 1.0)
        s *= 2
    hs = a * hprev[...] + b
    hprev[...] = hs[tl - 1:, :]
    y_ref[...] = (hs * jax.nn.gelu(zg_ref[...])).astype(y_ref.dtype)

    @pl.when(t == pl.num_programs(1) - 1)
    def _():
        hlast_ref[...] = hs[tl - 1:, :]
        conv_ref[...] = x[tl - (CONV_WIDTH - 1):, :]


def _lru_prompt(zx, zg, prm, j, batch, seq):
    tl = LRU_TILE
    nt = seq // tl
    row = pl.BlockSpec((tl, LRU_WIDTH), lambda b, t: (b * nt + t, 0))
    return pl.pallas_call(
        _lru_prompt_kernel,
        grid=(batch, nt),
        in_specs=[
            row, row,
            _layer_spec((CONV_WIDTH, LRU_WIDTH), j),
            _layer_spec((1, LRU_WIDTH), j),
            _layer_spec((LRU_WIDTH, 2 * LRU_WIDTH), j),
            _layer_spec((1, LRU_WIDTH), j),
            _layer_spec((1, LRU_WIDTH), j),
            _layer_spec((1, LRU_WIDTH), j),
        ],
        out_specs=(
            row,
            pl.BlockSpec((None, 1, LRU_WIDTH), lambda b, t: (b, 0, 0)),
            pl.BlockSpec((None, CONV_WIDTH - 1, LRU_WIDTH), lambda b, t: (b, 0, 0)),
        ),
        out_shape=(
            jax.ShapeDtypeStruct((batch * seq, LRU_WIDTH), BF16),
            jax.ShapeDtypeStruct((batch, 1, LRU_WIDTH), F32),
            jax.ShapeDtypeStruct((batch, CONV_WIDTH - 1, LRU_WIDTH), F32),
        ),
        scratch_shapes=[pltpu.VMEM((SUBLANES + tl, LRU_WIDTH), F32),
                        pltpu.VMEM((1, LRU_WIDTH), F32)],
        compiler_params=pltpu.CompilerParams(dimension_semantics=("parallel", "arbitrary"),
                                             vmem_limit_bytes=VMEM_LIMIT),
        name="lru_prompt",
    )(zx, zg, prm["conv_w"], prm["conv_b"], prm["w_gate_lru"], prm["b_rg"], prm["b_ig"],
      prm["lru_lambda"])


def _lru_sample_kernel(zx_ref, zg_ref, buf_ref, h0_ref, cw_ref, cb_ref, wgate_ref, brg_ref,
                       big_ref, lam_ref, y_ref, hlast_ref):
    nt = zx_ref.shape[0]
    nb = zx_ref.shape[1]
    xpad = [buf_ref[k] for k in range(CONV_WIDTH - 1)] + [zx_ref[k] for k in range(nt)]
    xcs = []
    for t in range(nt):
        xc = cb_ref[...] + xpad[t] * cw_ref[0:1, :]
        for k in range(1, CONV_WIDTH):
            xc = xc + xpad[t + k] * cw_ref[k:k + 1, :]
        xcs.append(xc)
    a, b = _lru_gates(jnp.concatenate(xcs, axis=0), wgate_ref, brg_ref, big_ref, lam_ref)
    h = h0_ref[...]
    for t in range(nt):
        h = a[t * nb:(t + 1) * nb] * h + b[t * nb:(t + 1) * nb]
        y_ref[t] = (h * jax.nn.gelu(zg_ref[t])).astype(y_ref.dtype)
    hlast_ref[...] = h


def _lru_sample(zx_t, zg_t, buf_t, h0, prm, j):
    nt, nb, _ = zx_t.shape
    return pl.pallas_call(
        _lru_sample_kernel,
        grid=(1,),
        in_specs=[
            _const_spec((nt, nb, LRU_WIDTH)), _const_spec((nt, nb, LRU_WIDTH)),
            _const_spec((CONV_WIDTH - 1, nb, LRU_WIDTH)), _const_spec((nb, LRU_WIDTH)),
            _layer_spec((CONV_WIDTH, LRU_WIDTH), j),
            _layer_spec((1, LRU_WIDTH), j),
            _layer_spec((LRU_WIDTH, 2 * LRU_WIDTH), j),
            _layer_spec((1, LRU_WIDTH), j),
            _layer_spec((1, LRU_WIDTH), j),
            _layer_spec((1, LRU_WIDTH), j),
        ],
        out_specs=(
            pl.BlockSpec((nt, nb, LRU_WIDTH), lambda i: (0, 0, 0)),
            pl.BlockSpec((nb, LRU_WIDTH), lambda i: (0, 0)),
        ),
        out_shape=(
            jax.ShapeDtypeStruct((nt, nb, LRU_WIDTH), BF16),
            jax.ShapeDtypeStruct((nb, LRU_WIDTH), F32),
        ),
        compiler_params=pltpu.CompilerParams(dimension_semantics=("arbitrary",),
                                             vmem_limit_bytes=VMEM_LIMIT),
        name="lru_sample",
    )(zx_t, zg_t, buf_t, h0, prm["conv_w"], prm["conv_b"], prm["w_gate_lru"], prm["b_rg"],
      prm["b_ig"], prm["lru_lambda"])


def _softmax_update(s, v, m_sc, l_sc, acc_sc):
    m_prev = m_sc[...]
    m_new = jnp.maximum(m_prev, jnp.max(s, axis=-1, keepdims=True))
    alpha = jnp.exp(m_prev - m_new)
    p = jnp.exp(s - m_new)
    l_sc[...] = alpha * l_sc[...] + jnp.sum(p, axis=-1, keepdims=True)
    acc_sc[...] = alpha * acc_sc[...] + _dot(p.astype(BF16), v)
    m_sc[...] = m_new


def _attn_prompt_kernel(q_ref, k_ref, wuv_ref, o_ref, m_sc, l_sc, acc_sc):
    i = pl.program_id(1)
    tq = q_ref.shape[1]
    rows = MLA_HEADS * tq
    q = q_ref[...].reshape(rows, QK_CAT)
    m_sc[...] = jnp.full(m_sc.shape, -jnp.inf, F32)
    l_sc[...] = jnp.zeros(l_sc.shape, F32)
    acc_sc[...] = jnp.zeros(acc_sc.shape, F32)

    def full_block(jb, carry):
        k = k_ref[pl.ds(pl.multiple_of(jb * tq, tq), tq), :]
        s = _dot_nt(q, k) * ATTN_SCALE
        _softmax_update(s, k[:, :KV_LORA], m_sc, l_sc, acc_sc)
        return carry

    lax.fori_loop(0, i, full_block, 0)

    k = k_ref[pl.ds(pl.multiple_of(i * tq, tq), tq), :]
    s = _dot_nt(q, k) * ATTN_SCALE
    q_pos = lax.broadcasted_iota(jnp.int32, (MLA_HEADS, tq, tq), 1).reshape(rows, tq)
    k_pos = lax.broadcasted_iota(jnp.int32, (rows, tq), 1)
    s = jnp.where(k_pos <= q_pos, s, -jnp.inf)
    _softmax_update(s, k[:, :KV_LORA], m_sc, l_sc, acc_sc)

    o = (acc_sc[...] / l_sc[...]).astype(BF16)
    outs = [_dot(o[hh * tq:(hh + 1) * tq], wuv_ref[hh]) for hh in range(MLA_HEADS)]
    o_ref[...] = jnp.concatenate(outs, axis=-1).astype(o_ref.dtype)


def _attn_prompt(qcat, kcat, prm, j, batch, seq):
    tq = ATTN_TILE
    nq = seq // tq
    rows = MLA_HEADS * tq
    return pl.pallas_call(
        _attn_prompt_kernel,
        grid=(batch, nq),
        in_specs=[
            pl.BlockSpec((MLA_HEADS, tq, QK_CAT), lambda b, i: (0, b * nq + i, 0)),
            pl.BlockSpec((seq, QK_CAT), lambda b, i: (b, 0)),
            _layer_spec((MLA_HEADS, KV_LORA, V_DIM), j),
        ],
        out_specs=pl.BlockSpec((tq, MLA_HEADS * V_DIM), lambda b, i: (b * nq + i, 0)),
        out_shape=jax.ShapeDtypeStruct((batch * seq, MLA_HEADS * V_DIM), BF16),
        scratch_shapes=[pltpu.VMEM((rows, 1), F32), pltpu.VMEM((rows, 1), F32),
                        pltpu.VMEM((rows, KV_LORA), F32)],
        compiler_params=pltpu.CompilerParams(dimension_semantics=("parallel", "arbitrary"),
                                             vmem_limit_bytes=VMEM_LIMIT),
        name="attn_prompt",
    )(qcat, kcat, prm["w_uv"])


def _attn_sample_kernel(pt_ref, q_ref, knew_ref, ckv_hbm, kpe_hbm, o_ref,
                        cbuf, pbuf, sems, m_sc, l_sc, acc_sc, *, layer, n_pages):
    b = pl.program_id(0)
    c = pl.program_id(1)
    n_chunks = pl.num_programs(1)
    g = b * n_chunks + c
    total = pl.num_programs(0) * n_chunks
    slot = g % 2

    def page_copies(gg, sl):
        base = (gg // n_chunks) * n_pages + (gg % n_chunks) * PAGES_PER_CHUNK
        copies = []
        for pg in range(PAGES_PER_CHUNK):
            page = pt_ref[base + pg]
            dst = pl.ds(pg * PAGE_SIZE, PAGE_SIZE)
            copies.append(pltpu.make_async_copy(ckv_hbm.at[layer, page], cbuf.at[sl, dst],
                                                sems.at[0, sl]))
            copies.append(pltpu.make_async_copy(kpe_hbm.at[layer, page], pbuf.at[sl, dst],
                                                sems.at[1, sl]))
        return copies

    @pl.when(g == 0)
    def _():
        for cp in page_copies(g, slot):
            cp.start()

    @pl.when(g + 1 < total)
    def _():
        for cp in page_copies(g + 1, 1 - slot):
            cp.start()

    @pl.when(c == 0)
    def _():
        m_sc[...] = jnp.full(m_sc.shape, -jnp.inf, F32)
        l_sc[...] = jnp.zeros(l_sc.shape, F32)
        acc_sc[...] = jnp.zeros(acc_sc.shape, F32)

    for cp in page_copies(g, slot):
        cp.wait()

    q = q_ref[...]
    kc = cbuf[slot].astype(BF16)
    kp = pbuf[slot].astype(BF16)
    s = (_dot_nt(q[:, :KV_LORA], kc) + _dot_nt(q[:, KV_LORA:KV_LORA + QK_ROPE], kp)) * ATTN_SCALE
    _softmax_update(s, kc, m_sc, l_sc, acc_sc)

    @pl.when(c == n_chunks - 1)
    def _():
        knew = knew_ref[...]
        s_new = _dot_nt(q, knew) * ATTN_SCALE
        t_row = lax.broadcasted_iota(jnp.int32, s_new.shape, 0) // MLA_HEADS
        t_key = lax.broadcasted_iota(jnp.int32, s_new.shape, 1)
        s_new = jnp.where(t_key <= t_row, s_new, -jnp.inf)
        _softmax_update(s_new, knew[:, :KV_LORA], m_sc, l_sc, acc_sc)
        o_ref[...] = (acc_sc[...] / l_sc[...]).astype(o_ref.dtype)


def _attn_sample(q_s, knew, cache_ckv, cache_kpe, page_table, j):
    nb, rows, _ = q_s.shape
    n_pages = page_table.shape[1]
    n_chunks = n_pages // PAGES_PER_CHUNK
    chunk_rows = PAGES_PER_CHUNK * PAGE_SIZE
    kernel = functools.partial(_attn_sample_kernel, layer=j, n_pages=n_pages)
    grid_spec = pltpu.PrefetchScalarGridSpec(
        num_scalar_prefetch=1,
        grid=(nb, n_chunks),
        in_specs=[
            pl.BlockSpec((None, rows, QK_CAT), lambda b, c, pt: (b, 0, 0)),
            pl.BlockSpec((None, knew.shape[1], QK_CAT), lambda b, c, pt: (b, 0, 0)),
            pl.BlockSpec(memory_space=pl.ANY),
            pl.BlockSpec(memory_space=pl.ANY),
        ],
        out_specs=pl.BlockSpec((None, rows, KV_LORA), lambda b, c, pt: (b, 0, 0)),
        scratch_shapes=[
            pltpu.VMEM((2, chunk_rows, KV_LORA), F32),
            pltpu.VMEM((2, chunk_rows, QK_ROPE), F32),
            pltpu.SemaphoreType.DMA((2, 2)),
            pltpu.VMEM((rows, 1), F32), pltpu.VMEM((rows, 1), F32),
            pltpu.VMEM((rows, KV_LORA), F32),
        ],
    )
    return pl.pallas_call(
        kernel,
        grid_spec=grid_spec,
        out_shape=jax.ShapeDtypeStruct((nb, rows, KV_LORA), BF16),
        compiler_params=pltpu.CompilerParams(dimension_semantics=("arbitrary", "arbitrary"),
                                             vmem_limit_bytes=VMEM_LIMIT),
        name="attn_sample",
    )(page_table.reshape(-1), q_s, knew, cache_ckv, cache_kpe)


def _uv_kernel(o_ref, wuv_ref, a_ref):
    outs = [_dot(o_ref[:, hh * KV_LORA:(hh + 1) * KV_LORA], wuv_ref[hh])
            for hh in range(MLA_HEADS)]
    a_ref[...] = jnp.concatenate(outs, axis=-1).astype(a_ref.dtype)


def _uv_project(o_lat, prm, j):
    n = o_lat.shape[0]
    return pl.pallas_call(
        _uv_kernel,
        grid=(1,),
        in_specs=[_const_spec((n, MLA_HEADS * KV_LORA)),
                  _layer_spec((MLA_HEADS, KV_LORA, V_DIM), j)],
        out_specs=pl.BlockSpec((n, MLA_HEADS * V_DIM), lambda i: (0, 0)),
        out_shape=jax.ShapeDtypeStruct((n, MLA_HEADS * V_DIM), BF16),
        compiler_params=pltpu.CompilerParams(dimension_semantics=("arbitrary",),
                                             vmem_limit_bytes=VMEM_LIMIT),
        name="uv_project",
    )(o_lat, prm["w_uv"])


def _ffn_and_gate(h1, p_ref, gffn_ref, wg_ref, wu_ref, wd_ref, gpe_ref, wpg_ref, wpe_ref):
    hn = _rms(h1, gffn_ref[...]).astype(BF16)
    act = (jax.nn.silu(_dot(hn, wg_ref[...])) * _dot(hn, wu_ref[...])).astype(BF16)
    h2 = h1 + _dot(act, wd_ref[...])
    gate = jax.nn.sigmoid(_dot(_rms(h2, gpe_ref[...]).astype(BF16), wpg_ref[...]))
    return h2 + _dot(p_ref[...].astype(BF16), wpe_ref[...]) * gate


def _post_ab_kernel(h_ref, attn_ref, lru_ref, p_ref, wout_ref, gffn_ref, wg_ref, wu_ref, wd_ref,
                    gpe_ref, wpg_ref, wpe_ref, out_ref):
    n_attn = MLA_HEADS * V_DIM
    mix = _dot(attn_ref[...], wout_ref[:n_attn, :]) + _dot(lru_ref[...], wout_ref[n_attn:, :])
    h1 = h_ref[...] + mix
    out_ref[...] = _ffn_and_gate(h1, p_ref, gffn_ref, wg_ref, wu_ref, wd_ref, gpe_ref, wpg_ref,
                                 wpe_ref)


def _post_c_kernel(h_ref, p_ref, gmix_ref, winc_ref, lng_ref, lnb_ref, ws_ref, bs_ref, woutc_ref,
                   gffn_ref, wg_ref, wu_ref, wd_ref, gpe_ref, wpg_ref, wpe_ref, gfin_ref,
                   out_ref, *rest, block_len, final):
    sp_sc = rest[-1]
    h = h_ref[...]
    tm = h.shape[0]
    z = jax.nn.gelu(_dot(_rms(h, gmix_ref[...]).astype(BF16), winc_ref[...]))
    u = z[:, :GMLP_WIDTH]
    v = z[:, GMLP_WIDTH:]
    vc = v - jnp.mean(v, axis=-1, keepdims=True)
    v = vc * lax.rsqrt(jnp.mean(vc * vc, axis=-1, keepdims=True) + EPS) * lng_ref[...] + lnb_ref[...]
    if len(rest) == 2:
        rest[0][...] = v
    vb = v.astype(BF16)
    t_idx = lax.broadcasted_iota(jnp.int32, (CHUNK, CHUNK), 0)
    s_idx = lax.broadcasted_iota(jnp.int32, (CHUNK, CHUNK), 1)
    mask = (s_idx <= t_idx) & ((s_idx // block_len) == (t_idx // block_len))
    for gg in range(GMLP_GROUPS):
        wm = jnp.where(mask, ws_ref[gg], 0.0).astype(BF16)
        cols = slice(gg * GMLP_GROUP_DIM, (gg + 1) * GMLP_GROUP_DIM)
        for cc in range(tm // CHUNK):
            rows = slice(cc * CHUNK, (cc + 1) * CHUNK)
            sp_sc[rows, cols] = _dot(wm, vb[rows, cols]) + bs_ref[:, cols]
    mix = _dot((u * sp_sc[...]).astype(BF16), woutc_ref[...])
    h3 = _ffn_and_gate(h + mix, p_ref, gffn_ref, wg_ref, wu_ref, wd_ref, gpe_ref, wpg_ref, wpe_ref)
    out_ref[...] = _rms(h3, gfin_ref[...]) if final else h3


def _ffn_specs(i):
    return [
        _layer_spec((1, D_MODEL), i),
        _layer_spec((D_MODEL, FFN_HIDDEN), i),
        _layer_spec((D_MODEL, FFN_HIDDEN), i),
        _layer_spec((FFN_HIDDEN, D_MODEL), i),
        _layer_spec((1, D_MODEL), i),
        _layer_spec((D_MODEL, D_MODEL), i),
        _layer_spec((PLE_DIM, D_MODEL), i),
    ]


def _ffn_args(prm):
    return (prm["g_ffn"], prm["w_gate"], prm["w_up"], prm["w_down"], prm["g_pe"], prm["w_pg"],
            prm["w_pe"])


def _post_ab(h, attn, lru_y, p, prm, i):
    n = h.shape[0]
    tm = ROW_TILE
    j = i // 2
    row = lambda w: pl.BlockSpec((tm, w), lambda r: (r, 0))
    return pl.pallas_call(
        _post_ab_kernel,
        grid=(n // tm,),
        in_specs=[
            row(D_MODEL), row(MLA_HEADS * V_DIM), row(LRU_WIDTH),
            pl.BlockSpec((None, tm, PLE_DIM), lambda r: (i, r, 0)),
            _layer_spec((MLA_HEADS * V_DIM + LRU_WIDTH, D_MODEL), j),
        ] + _ffn_specs(i),
        out_specs=row(D_MODEL),
        out_shape=jax.ShapeDtypeStruct((n, D_MODEL), F32),
        compiler_params=pltpu.CompilerParams(dimension_semantics=("parallel",),
                                             vmem_limit_bytes=VMEM_LIMIT),
        name="post_ab",
    )(h, attn, lru_y, p, prm["w_out_ab"], *_ffn_args(prm))


def _post_c(h, p, prm, i, ws_tiled, bs_full, block_len, emit_v):
    n = h.shape[0]
    tm = ROW_TILE
    j = i // 2
    final = i == DEPTH - 1
    row = lambda w: pl.BlockSpec((tm, w), lambda r: (r, 0))
    kernel = functools.partial(_post_c_kernel, block_len=block_len, final=final)
    out_specs = [row(D_MODEL)] + ([row(GMLP_WIDTH)] if emit_v else [])
    out_shape = [jax.ShapeDtypeStruct((n, D_MODEL), F32)]
    if emit_v:
        out_shape.append(jax.ShapeDtypeStruct((n, GMLP_WIDTH), F32))
    outs = pl.pallas_call(
        kernel,
        grid=(n // tm,),
        in_specs=[
            row(D_MODEL),
            pl.BlockSpec((None, tm, PLE_DIM), lambda r: (i, r, 0)),
            _layer_spec((1, D_MODEL), i),
            _layer_spec((D_MODEL, 2 * GMLP_WIDTH), j),
            _layer_spec((1, GMLP_WIDTH), j),
            _layer_spec((1, GMLP_WIDTH), j),
            _layer_spec((GMLP_GROUPS, CHUNK, CHUNK), j),
            _layer_spec((CHUNK, GMLP_WIDTH), j),
            _layer_spec((GMLP_WIDTH, D_MODEL), j),
        ] + _ffn_specs(i) + [_const_spec((1, D_MODEL))],
        out_specs=out_specs,
        out_shape=out_shape,
        scratch_shapes=[pltpu.VMEM((tm, GMLP_WIDTH), F32)],
        compiler_params=pltpu.CompilerParams(dimension_semantics=("parallel",),
                                             vmem_limit_bytes=VMEM_LIMIT),
        name="post_c",
    )(h, p, prm["g_mix"], prm["w_in_c"], prm["ln_g_c"], prm["ln_b_c"], ws_tiled, bs_full,
      prm["w_out_c"], *_ffn_args(prm), prm["g_final"])
    return outs[0], (outs[1] if emit_v else None)


def _rope_tables(pos):
    inv = jnp.exp(-math.log(ROPE_THETA) * jnp.arange(ROPE_HALF, dtype=F32) / ROPE_HALF)
    ang = pos.astype(F32)[:, None] * inv[None, :]
    cos, sin = jnp.cos(ang), jnp.sin(ang)
    zeros = jnp.zeros((pos.shape[0], LANES - QK_ROPE), F32)
    return (jnp.concatenate([cos, cos, zeros], axis=1),
            jnp.concatenate([-sin, sin, zeros], axis=1))


def _prepare_params(g_mix, g_ffn, g_pe, g_final, w_in_ab, g_qnorm, g_kvnorm, w_uq, w_uk, w_uv,
                    conv_w, conv_b, w_rg, b_rg, w_ig, b_ig, lru_lambda, w_out_ab, w_in_c, ln_g_c,
                    ln_b_c, w_s, b_s, w_out_c, w_gate, w_up, w_down, w_pe, w_pg):
    n_ab = w_in_ab.shape[0]
    o1 = Q_LORA + KV_LORA
    o2 = o1 + QK_ROPE
    w_in = jnp.concatenate(
        [w_in_ab[:, :, :o1], w_in_ab[:, :, o2:], w_in_ab[:, :, o1:o2],
         jnp.zeros((n_ab, D_MODEL, LANES - QK_ROPE), F32)], axis=2)
    uq = w_uq.reshape(n_ab, Q_LORA, MLA_HEADS, QK_NOPE + QK_ROPE)
    uq_nope = uq[..., :QK_NOPE].reshape(n_ab, Q_LORA, MLA_HEADS * QK_NOPE)
    uq_rope = jnp.pad(uq[..., QK_NOPE:], ((0, 0), (0, 0), (0, 0), (0, LANES - QK_ROPE)))
    uq_rope = uq_rope.reshape(n_ab, Q_LORA, MLA_HEADS * LANES)

    def block_diag(w):
        eye = jnp.eye(LRU_HEADS, dtype=F32)
        return jnp.einsum("jhab,hg->jhagb", w, eye).reshape(n_ab, LRU_WIDTH, LRU_WIDTH)

    vec = lambda a: a[:, None, :]
    return {
        "g_mix": vec(g_mix), "g_ffn": vec(g_ffn), "g_pe": vec(g_pe), "g_final": g_final[None, :],
        "w_in_ab": w_in.astype(BF16), "g_qnorm": vec(g_qnorm), "g_kvnorm": vec(g_kvnorm),
        "w_uq": jnp.concatenate([uq_nope, uq_rope], axis=2).astype(BF16),
        "w_uk": w_uk.astype(BF16), "w_uv": w_uv.astype(BF16),
        "conv_w": conv_w, "conv_b": vec(conv_b),
        "w_gate_lru": jnp.concatenate([block_diag(w_rg), block_diag(w_ig)], axis=2).astype(BF16),
        "b_rg": vec(b_rg), "b_ig": vec(b_ig), "lru_lambda": vec(lru_lambda),
        "w_out_ab": w_out_ab.astype(BF16),
        "w_in_c": w_in_c.astype(BF16), "ln_g_c": vec(ln_g_c), "ln_b_c": vec(ln_b_c),
        "w_s": w_s, "b_s": b_s, "w_out_c": w_out_c.astype(BF16),
        "w_gate": w_gate.astype(BF16), "w_up": w_up.astype(BF16), "w_down": w_down.astype(BF16),
        "w_pe": w_pe.astype(BF16), "w_pg": w_pg.astype(BF16),
    }


def _gmlp_spatial(prm, block_len):
    reps = CHUNK // block_len
    ws = jnp.tile(prm["w_s"][:, :, :block_len, :block_len], (1, 1, reps, reps))
    bs = jnp.tile(prm["b_s"][:, :, :block_len], (1, 1, reps))
    bs = jnp.repeat(jnp.swapaxes(bs, 1, 2), GMLP_GROUP_DIM, axis=2)
    return ws, bs


def _trunk(x, p, cs, sn, n_pos_tiles, prm, past, batch, seq):
    n = batch * seq
    h = x.reshape(n, D_MODEL)
    p = p.reshape(DEPTH, n, PLE_DIM)
    block_len = min(seq, CHUNK)
    ws_tiled, bs_full = _gmlp_spatial(prm, block_len)
    ckv, kpe, lru, conv, vrows = [], [], [], [], []
    for i in range(DEPTH):
        j = i // 2
        if i % 2 == 0:
            qcat, kcat, c_kv, k_pe, zx, zg = _in_even(h, cs, sn, prm, j, n_pos_tiles)
            if past is None:
                attn = _attn_prompt(qcat, kcat, prm, j, batch, seq)
                lru_y, h_last, buf = _lru_prompt(zx, zg, prm, j, batch, seq)
                h_last = h_last.reshape(batch, LRU_WIDTH)
            else:
                q_s = qcat.reshape(MLA_HEADS, batch, seq, QK_CAT).transpose(1, 2, 0, 3)
                q_s = q_s.reshape(batch, seq * MLA_HEADS, QK_CAT)
                knew = jnp.pad(kcat.reshape(batch, seq, QK_CAT),
                               ((0, 0), (0, 2 * SUBLANES - seq), (0, 0)))
                o_lat = _attn_sample(q_s, knew, past["cache_ckv"], past["cache_kpe"],
                                     past["page_table"], j)
                attn = _uv_project(o_lat.reshape(n, MLA_HEADS * KV_LORA), prm, j)
                to_t = lambda a: a.reshape(batch, seq, LRU_WIDTH).transpose(1, 0, 2)
                y_t, h_last = _lru_sample(to_t(zx), to_t(zg),
                                          past["state_conv"][j].transpose(1, 0, 2),
                                          past["state_lru"][j], prm, j)
                lru_y = y_t.transpose(1, 0, 2).reshape(n, LRU_WIDTH)
                buf = jnp.concatenate([past["state_conv"][j], zx.reshape(batch, seq, LRU_WIDTH)],
                                      axis=1)[:, seq:]
            h = _post_ab(h, attn, lru_y, p, prm, i)
            ckv.append(c_kv.reshape(batch, seq, KV_LORA))
            kpe.append(k_pe.reshape(batch, seq, QK_ROPE))
            lru.append(h_last)
            conv.append(buf)
        else:
            h, v = _post_c(h, p, prm, i, ws_tiled, bs_full, block_len, past is not None)
            if v is not None:
                vrows.append(v.reshape(batch, seq, GMLP_WIDTH))
    return h.reshape(batch, seq, D_MODEL), ckv, kpe, lru, conv, vrows


def kernel(x_prompt, x_sample, cache_ckv, cache_kpe, state_lru, state_conv, page_table, p_prompt, p_sample, g_mix, g_ffn, g_pe, g_final, w_in_ab, g_qnorm, g_kvnorm, w_uq, w_uk, w_uv, conv_w, conv_b, w_rg, b_rg, w_ig, b_ig, lru_lambda, w_out_ab, w_in_c, ln_g_c, ln_b_c, w_s, b_s, w_out_c, w_gate, w_up, w_down, w_pe, w_pg):
    prm = _prepare_params(g_mix, g_ffn, g_pe, g_final, w_in_ab, g_qnorm, g_kvnorm, w_uq, w_uk,
                          w_uv, conv_w, conv_b, w_rg, b_rg, w_ig, b_ig, lru_lambda, w_out_ab,
                          w_in_c, ln_g_c, ln_b_c, w_s, b_s, w_out_c, w_gate, w_up, w_down, w_pe,
                          w_pg)
    batch, seq, _ = x_prompt.shape
    dec_batch, dec_seq, _ = x_sample.shape
    past_len = page_table.shape[1] * PAGE_SIZE

    cs_p, sn_p = _rope_tables(jnp.arange(seq, dtype=jnp.int32))
    y_prompt, ckv_p, kpe_p, lru_p, conv_p, _ = _trunk(
        x_prompt, p_prompt, cs_p, sn_p, seq // ROW_TILE, prm, None, batch, seq)

    cs_s, sn_s = _rope_tables(past_len + jnp.arange(dec_seq, dtype=jnp.int32))
    reps = ROW_TILE // dec_seq
    cs_s, sn_s = jnp.tile(cs_s, (reps, 1)), jnp.tile(sn_s, (reps, 1))
    past = {"cache_ckv": cache_ckv, "cache_kpe": cache_kpe, "state_lru": state_lru,
            "state_conv": state_conv, "page_table": page_table}
    y_sample, ckv_s, kpe_s, lru_s, conv_s, v_s = _trunk(
        x_sample, p_sample, cs_s, sn_s, 1, prm, past, dec_batch, dec_seq)

    return (y_prompt, y_sample,
            jnp.stack(ckv_p), jnp.stack(kpe_p), jnp.stack(lru_p), jnp.stack(conv_p),
            jnp.stack(ckv_s), jnp.stack(kpe_s), jnp.stack(lru_s), jnp.stack(conv_s),
            jnp.stack(v_s))
```

```python
import functools
import math

import jax
import jax.numpy as jnp
from jax import lax
from jax.experimental import pallas as pl
from jax.experimental.pallas import tpu as pltpu

F32 = jnp.float32
BF16 = jnp.bfloat16

D_MODEL = 1024
DEPTH = 4
PAGE_SIZE = 128
MLA_HEADS = 8
Q_LORA = 384
KV_LORA = 256
QK_NOPE = 64
QK_ROPE = 32
ROPE_HALF = QK_ROPE // 2
V_DIM = 64
ROPE_THETA = 10000.0
ATTN_SCALE = 1.0 / math.sqrt(QK_NOPE + QK_ROPE)
LRU_WIDTH = 512
LRU_HEADS = 8
LRU_HEAD_DIM = LRU_WIDTH // LRU_HEADS
CONV_WIDTH = 4
LRU_C = 8.0
CHUNK = 128
GMLP_WIDTH = 1024
GMLP_GROUPS = 8
GMLP_GROUP_DIM = GMLP_WIDTH // GMLP_GROUPS
FFN_HIDDEN = -(-8 * D_MODEL // (3 * 256)) * 256
PLE_DIM = 256
EPS = 1e-6

LANES = 128
SUBLANES = 8
QK_CAT = KV_LORA + LANES
IN_AB_PAD = Q_LORA + KV_LORA + 2 * LRU_WIDTH + LANES
VMEM_LIMIT = 56 * 1024 * 1024

ROW_TILE = 256
ATTN_TILE = 256
LRU_TILE = 256


def _const_spec(shape):
    nd = len(shape)
    return pl.BlockSpec(shape, lambda *_: (0,) * nd, pipeline_mode=pl.Buffered(1))


def _layer_spec(shape, layer):
    nd = len(shape)
    return pl.BlockSpec((None,) + tuple(shape), lambda *_: (layer,) + (0,) * nd,
                        pipeline_mode=pl.Buffered(1))


def _rms(x, g):
    return x * lax.rsqrt(jnp.mean(x * x, axis=-1, keepdims=True) + EPS) * g


def _dot(a, b):
    return jnp.dot(a, b, preferred_element_type=F32)


def _dot_nt(a, b):
    return lax.dot_general(a, b, (((1,), (1,)), ((), ())), preferred_element_type=F32)


def _rope_block(blk, cs, sn):
    lane = lax.broadcasted_iota(jnp.int32, blk.shape, 1)
    swapped = jnp.where(lane < QK_NOPE + ROPE_HALF,
                        pltpu.roll(blk, LANES - ROPE_HALF, 1),
                        pltpu.roll(blk, ROPE_HALF, 1))
    return blk * cs + swapped * sn


def _in_even_common(h_ref, csk_ref, snk_ref, gmix_ref, win_ref, gq_ref, wuq_ref, gkv_ref,
                    ckv_ref, kpe_ref, zx_ref, zg_ref):
    xn = _rms(h_ref[...], gmix_ref[...]).astype(BF16)
    z = _dot(xn, win_ref[...])
    o_kv = Q_LORA
    o_x = o_kv + KV_LORA
    o_g = o_x + LRU_WIDTH
    o_pe = o_g + LRU_WIDTH
    zx_ref[...] = z[:, o_x:o_g]
    zg_ref[...] = z[:, o_g:o_pe]
    ckv = _rms(z[:, o_kv:o_x], gkv_ref[...])
    kpe_blk = _rope_block(z[:, o_pe:o_pe + LANES], csk_ref[...], snk_ref[...])
    ckv_ref[...] = ckv
    kpe_ref[...] = kpe_blk[:, QK_NOPE:QK_NOPE + QK_ROPE]
    qn = _rms(z[:, :Q_LORA], gq_ref[...]).astype(BF16)
    q = _dot(qn, wuq_ref[...])
    return ckv, kpe_blk, q


def _in_even_sample_kernel(h_ref, csk_ref, snk_ref, gmix_ref, win_ref, gq_ref, wuq_ref, gkv_ref,
                           wuk_ref, qcat_ref, kcat_ref, ckv_ref, kpe_ref, zx_ref, zg_ref):
    ckv, kpe_blk, q = _in_even_common(h_ref, csk_ref, snk_ref, gmix_ref, win_ref, gq_ref, wuq_ref,
                                      gkv_ref, ckv_ref, kpe_ref, zx_ref, zg_ref)
    kcat_ref[:, :KV_LORA] = ckv.astype(BF16)
    kcat_ref[:, KV_LORA:] = kpe_blk.astype(BF16)
    for hh in range(MLA_HEADS):
        blk = q[:, hh * LANES:(hh + 1) * LANES]
        qcat_ref[hh, :, :KV_LORA] = _dot(blk[:, :QK_NOPE].astype(BF16), wuk_ref[hh]).astype(BF16)
        qcat_ref[hh, :, KV_LORA:] = _rope_block(blk, csk_ref[...], snk_ref[...]).astype(BF16)


def _in_even_prompt_kernel(h_ref, csk_ref, snk_ref, gmix_ref, win_ref, gq_ref, wuq_ref, gkv_ref,
                           csq_ref, snq_ref, wukt_ref, wuvc_ref,
                           q_ref, k_ref, v_ref, ckv_ref, kpe_ref, zx_ref, zg_ref):
    ckv, kpe_blk, q = _in_even_common(h_ref, csk_ref, snk_ref, gmix_ref, win_ref, gq_ref, wuq_ref,
                                      gkv_ref, ckv_ref, kpe_ref, zx_ref, zg_ref)
    cb = ckv.astype(BF16)
    k_nope = _dot(cb, wukt_ref[...])
    v = _dot(cb, wuvc_ref[...])
    for hh in range(MLA_HEADS):
        blk = slice(hh * LANES, (hh + 1) * LANES)
        q_ref[hh] = _rope_block(q[:, blk], csq_ref[...], snq_ref[...]).astype(BF16)
        k_ref[hh] = (k_nope[:, blk] + kpe_blk).astype(BF16)
        v_ref[hh] = v[:, hh * V_DIM:(hh + 1) * V_DIM].astype(BF16)


def _in_even(h, tables, prm, j, n_pos_tiles, absorbed):
    n = h.shape[0]
    tm = ROW_TILE
    row = lambda w: pl.BlockSpec((tm, w), lambda r: (r, 0))
    heads = lambda w: pl.BlockSpec((MLA_HEADS, tm, w), lambda r: (0, r, 0))
    pos = pl.BlockSpec((tm, LANES), lambda r: (r % n_pos_tiles, 0))
    in_specs = [
        row(D_MODEL), pos, pos,
        _layer_spec((1, D_MODEL), 2 * j),
        _layer_spec((D_MODEL, IN_AB_PAD), j),
        _layer_spec((1, Q_LORA), j),
        _layer_spec((Q_LORA, MLA_HEADS * LANES), j),
        _layer_spec((1, KV_LORA), j),
    ]
    args = [h, tables["csk"], tables["snk"], prm["g_mix"], prm["w_in_ab"], prm["g_qnorm"],
            prm["w_uq"], prm["g_kvnorm"]]
    tail_specs = (row(KV_LORA), row(QK_ROPE), row(LRU_WIDTH), row(LRU_WIDTH))
    tail_shapes = (
        jax.ShapeDtypeStruct((n, KV_LORA), F32),
        jax.ShapeDtypeStruct((n, QK_ROPE), F32),
        jax.ShapeDtypeStruct((n, LRU_WIDTH), F32),
        jax.ShapeDtypeStruct((n, LRU_WIDTH), F32),
    )
    if absorbed:
        body = _in_even_sample_kernel
        in_specs += [_layer_spec((MLA_HEADS, QK_NOPE, KV_LORA), j)]
        args += [prm["w_uk"]]
        out_specs = (heads(QK_CAT), row(QK_CAT)) + tail_specs
        out_shape = (jax.ShapeDtypeStruct((MLA_HEADS, n, QK_CAT), BF16),
                     jax.ShapeDtypeStruct((n, QK_CAT), BF16)) + tail_shapes
    else:
        body = _in_even_prompt_kernel
        in_specs += [pos, pos,
                     _layer_spec((KV_LORA, MLA_HEADS * LANES), j),
                     _layer_spec((KV_LORA, MLA_HEADS * V_DIM), j)]
        args += [tables["csq"], tables["snq"], prm["w_uk_t"], prm["w_uv_cat"]]
        out_specs = (heads(LANES), heads(LANES), heads(V_DIM)) + tail_specs
        out_shape = (jax.ShapeDtypeStruct((MLA_HEADS, n, LANES), BF16),
                     jax.ShapeDtypeStruct((MLA_HEADS, n, LANES), BF16),
                     jax.ShapeDtypeStruct((MLA_HEADS, n, V_DIM), BF16)) + tail_shapes
    return pl.pallas_call(
        body,
        grid=(n // tm,),
        in_specs=in_specs,
        out_specs=out_specs,
        out_shape=out_shape,
        compiler_params=pltpu.CompilerParams(dimension_semantics=("parallel",),
                                             vmem_limit_bytes=VMEM_LIMIT),
        name="in_even_absorbed" if absorbed else "in_even_heads",
    )(*args)


def _lru_gates(xc, wgate_ref, brg_ref, big_ref, lam_ref):
    g = _dot(xc.astype(BF16), wgate_ref[...])
    r = jax.nn.sigmoid(g[:, :LRU_WIDTH] + brg_ref[...])
    ig = jax.nn.sigmoid(g[:, LRU_WIDTH:] + big_ref[...])
    x = -lam_ref[...]
    softplus = jnp.maximum(x, 0.0) + jnp.log1p(jnp.exp(-jnp.abs(x)))
    log_a = -LRU_C * r * softplus
    a = jnp.exp(log_a)
    th = jnp.tanh(log_a)
    bx = jnp.sqrt(-2.0 * th / (1.0 - th)) * ig * xc
    return a, bx


def _shift_rows(x, s, fill):
    return jnp.concatenate([jnp.full((s, x.shape[1]), fill, x.dtype), x[:-s]], axis=0)


def _lru_prompt_kernel(zx_ref, zg_ref, cw_ref, cb_ref, wgate_ref, brg_ref, big_ref, lam_ref,
                       y_ref, hlast_ref, conv_ref, xbuf, hprev):
    t = pl.program_id(1)
    tl = zx_ref.shape[0]
    pad = SUBLANES

    @pl.when(t == 0)
    def _():
        xbuf[:pad, :] = jnp.zeros((pad, LRU_WIDTH), F32)
        hprev[...] = jnp.zeros_like(hprev)

    x = zx_ref[...]
    xbuf[pad:, :] = x
    xc = cb_ref[...] + xbuf[pad - 3:pad - 3 + tl, :] * cw_ref[0:1, :]
    xc = xc + xbuf[pad - 2:pad - 2 + tl, :] * cw_ref[1:2, :]
    xc = xc + xbuf[pad - 1:pad - 1 + tl, :] * cw_ref[2:3, :]
    xc = xc + x * cw_ref[3:4, :]
    xbuf[:pad, :] = x[tl - pad:, :]

    a, b = _lru_gates(xc, wgate_ref, brg_ref, big_ref, lam_ref)
    s = 1
    while s < tl:
        b = a * _shift_rows(b, s, 0.0) + b
        a = a * _shift_rows(a, s, 1.0)
        s *= 2
    hs = a * hprev[...] + b
    hprev[...] = hs[tl - 1:, :]
    y_ref[...] = (hs * jax.nn.gelu(zg_ref[...])).astype(y_ref.dtype)

    @pl.when(t == pl.num_programs(1) - 1)
    def _():
        hlast_ref[...] = hs[tl - 1:, :]
        conv_ref[...] = x[tl - (CONV_WIDTH - 1):, :]


def _lru_prompt(zx, zg, prm, j, batch, seq):
    tl = LRU_TILE
    nt = seq // tl
    row = pl.BlockSpec((tl, LRU_WIDTH), lambda b, t: (b * nt + t, 0))
    return pl.pallas_call(
        _lru_prompt_kernel,
        grid=(batch, nt),
        in_specs=[
            row, row,
            _layer_spec((CONV_WIDTH, LRU_WIDTH), j),
            _layer_spec((1, LRU_WIDTH), j),
            _layer_spec((LRU_WIDTH, 2 * LRU_WIDTH), j),
            _layer_spec((1, LRU_WIDTH), j),
            _layer_spec((1, LRU_WIDTH), j),
            _layer_spec((1, LRU_WIDTH), j),
        ],
        out_specs=(
            row,
            pl.BlockSpec((None, 1, LRU_WIDTH), lambda b, t: (b, 0, 0)),
            pl.BlockSpec((None, CONV_WIDTH - 1, LRU_WIDTH), lambda b, t: (b, 0, 0)),
        ),
        out_shape=(
            jax.ShapeDtypeStruct((batch * seq, LRU_WIDTH), BF16),
            jax.ShapeDtypeStruct((batch, 1, LRU_WIDTH), F32),
            jax.ShapeDtypeStruct((batch, CONV_WIDTH - 1, LRU_WIDTH), F32),
        ),
        scratch_shapes=[pltpu.VMEM((SUBLANES + tl, LRU_WIDTH), F32),
                        pltpu.VMEM((1, LRU_WIDTH), F32)],
        compiler_params=pltpu.CompilerParams(dimension_semantics=("parallel", "arbitrary"),
                                             vmem_limit_bytes=VMEM_LIMIT),
        name="lru_prompt",
    )(zx, zg, prm["conv_w"], prm["conv_b"], prm["w_gate_lru"], prm["b_rg"], prm["b_ig"],
      prm["lru_lambda"])


def _lru_sample_kernel(zx_ref, zg_ref, buf_ref, h0_ref, cw_ref, cb_ref, wgate_ref, brg_ref,
                       big_ref, lam_ref, y_ref, hlast_ref):
    nt = zx_ref.shape[0]
    nb = zx_ref.shape[1]
    xpad = [buf_ref[k] for k in range(CONV_WIDTH - 1)] + [zx_ref[k] for k in range(nt)]
    xcs = []
    for t in range(nt):
        xc = cb_ref[...] + xpad[t] * cw_ref[0:1, :]
        for k in range(1, CONV_WIDTH):
            xc = xc + xpad[t + k] * cw_ref[k:k + 1, :]
        xcs.append(xc)
    a, b = _lru_gates(jnp.concatenate(xcs, axis=0), wgate_ref, brg_ref, big_ref, lam_ref)
    h = h0_ref[...]
    for t in range(nt):
        h = a[t * nb:(t + 1) * nb] * h + b[t * nb:(t + 1) * nb]
        y_ref[t] = (h * jax.nn.gelu(zg_ref[t])).astype(y_ref.dtype)
    hlast_ref[...] = h


def _lru_sample(zx_t, zg_t, buf_t, h0, prm, j):
    nt, nb, _ = zx_t.shape
    return pl.pallas_call(
        _lru_sample_kernel,
        grid=(1,),
        in_specs=[
            _const_spec((nt, nb, LRU_WIDTH)), _const_spec((nt, nb, LRU_WIDTH)),
            _const_spec((CONV_WIDTH - 1, nb, LRU_WIDTH)), _const_spec((nb, LRU_WIDTH)),
            _layer_spec((CONV_WIDTH, LRU_WIDTH), j),
            _layer_spec((1, LRU_WIDTH), j),
            _layer_spec((LRU_WIDTH, 2 * LRU_WIDTH), j),
            _layer_spec((1, LRU_WIDTH), j),
            _layer_spec((1, LRU_WIDTH), j),
            _layer_spec((1, LRU_WIDTH), j),
        ],
        out_specs=(
            pl.BlockSpec((nt, nb, LRU_WIDTH), lambda i: (0, 0, 0)),
            pl.BlockSpec((nb, LRU_WIDTH), lambda i: (0, 0)),
        ),
        out_shape=(
            jax.ShapeDtypeStruct((nt, nb, LRU_WIDTH), BF16),
            jax.ShapeDtypeStruct((nb, LRU_WIDTH), F32),
        ),
        compiler_params=pltpu.CompilerParams(dimension_semantics=("arbitrary",),
                                             vmem_limit_bytes=VMEM_LIMIT),
        name="lru_sample",
    )(zx_t, zg_t, buf_t, h0, prm["conv_w"], prm["conv_b"], prm["w_gate_lru"], prm["b_rg"],
      prm["b_ig"], prm["lru_lambda"])


def _attn_prompt_kernel(q_ref, k_ref, v_ref, o_ref, m_sc, l_sc, acc_sc):
    i = pl.program_id(1)
    tq = q_ref.shape[1]
    lane_reps = tq // LANES
    m_sc[...] = jnp.full(m_sc.shape, -jnp.inf, F32)
    l_sc[...] = jnp.zeros(l_sc.shape, F32)
    acc_sc[...] = jnp.zeros(acc_sc.shape, F32)

    def key_block(start, masked):
        keys = pl.ds(start, tq)
        if masked:
            causal = (lax.broadcasted_iota(jnp.int32, (tq, tq), 1)
                      <= lax.broadcasted_iota(jnp.int32, (tq, tq), 0))
        scores = [_dot_nt(q_ref[hh], k_ref[hh, keys, :]) for hh in range(MLA_HEADS)]
        for hh in range(MLA_HEADS):
            s = scores[hh]
            if masked:
                s = jnp.where(causal, s, -jnp.inf)
            m_prev = m_sc[hh]
            m_new = jnp.maximum(m_prev, jnp.max(s, axis=-1, keepdims=True))
            alpha = jnp.exp2(m_prev - m_new)
            p = jnp.exp2(s - jnp.concatenate([m_new] * lane_reps, axis=1))
            l_sc[hh] = alpha * l_sc[hh] + jnp.sum(p, axis=-1, keepdims=True)
            acc_sc[hh] = (alpha[:, :V_DIM] * acc_sc[hh]
                          + _dot(p.astype(BF16), v_ref[hh, keys, :]))
            m_sc[hh] = m_new

    def full_block(jb, carry):
        key_block(pl.multiple_of(jb * tq, tq), False)
        return carry

    lax.fori_loop(0, i, full_block, 0)
    key_block(pl.multiple_of(i * tq, tq), True)

    outs = [acc_sc[hh] / l_sc[hh][:, :V_DIM] for hh in range(MLA_HEADS)]
    o_ref[...] = jnp.concatenate(outs, axis=-1).astype(o_ref.dtype)


def _attn_prompt(q, k, v, batch, seq):
    tq = ATTN_TILE
    nq = seq // tq
    return pl.pallas_call(
        _attn_prompt_kernel,
        grid=(batch, nq),
        in_specs=[
            pl.BlockSpec((MLA_HEADS, tq, LANES), lambda b, i: (0, b * nq + i, 0)),
            pl.BlockSpec((MLA_HEADS, seq, LANES), lambda b, i: (0, b, 0)),
            pl.BlockSpec((MLA_HEADS, seq, V_DIM), lambda b, i: (0, b, 0)),
        ],
        out_specs=pl.BlockSpec((tq, MLA_HEADS * V_DIM), lambda b, i: (b * nq + i, 0)),
        out_shape=jax.ShapeDtypeStruct((batch * seq, MLA_HEADS * V_DIM), BF16),
        scratch_shapes=[pltpu.VMEM((MLA_HEADS, tq, LANES), F32),
                        pltpu.VMEM((MLA_HEADS, tq, LANES), F32),
                        pltpu.VMEM((MLA_HEADS, tq, V_DIM), F32)],
        compiler_params=pltpu.CompilerParams(dimension_semantics=("parallel", "arbitrary"),
                                             vmem_limit_bytes=VMEM_LIMIT),
        name="attn_prompt",
    )(q, k, v)


def _attn_sample_kernel(pt_ref, q_ref, knew_ref, ckv_hbm, kpe_hbm, o_ref, cbuf, pbuf, sems, *,
                        layer, n_pages):
    b = pl.program_id(0)
    nb = pl.num_programs(0)
    slot = b % 2

    def page_copies(bb, sl):
        copies = []
        for pg in range(n_pages):
            page = pt_ref[bb * n_pages + pg]
            rows = pl.ds(pg * PAGE_SIZE, PAGE_SIZE)
            copies.append(pltpu.make_async_copy(ckv_hbm.at[layer, page], cbuf.at[sl, rows],
                                                sems.at[0, sl]))
            copies.append(pltpu.make_async_copy(kpe_hbm.at[layer, page], pbuf.at[sl, :, rows],
                                                sems.at[1, sl]))
        return copies

    @pl.when(b == 0)
    def _():
        for cp in page_copies(b, slot):
            cp.start()

    @pl.when(b + 1 < nb)
    def _():
        for cp in page_copies(b + 1, 1 - slot):
            cp.start()

    for cp in page_copies(b, slot):
        cp.wait()

    q = q_ref[...]
    kc = cbuf[slot].astype(BF16)
    kp = pbuf[slot].astype(BF16)
    o_pe = KV_LORA + QK_NOPE
    s_past = (_dot_nt(q[:, :KV_LORA], kc) + _dot(q[:, o_pe:o_pe + QK_ROPE], kp)) * ATTN_SCALE
    knew = knew_ref[...]
    s_new = _dot_nt(q, knew) * ATTN_SCALE
    t_row = lax.broadcasted_iota(jnp.int32, s_new.shape, 0) // MLA_HEADS
    t_key = lax.broadcasted_iota(jnp.int32, s_new.shape, 1)
    s_new = jnp.where(t_key <= t_row, s_new, -jnp.inf)
    m = jnp.maximum(jnp.max(s_past, axis=-1, keepdims=True), jnp.max(s_new, axis=-1, keepdims=True))
    p_past = jnp.exp(s_past - m)
    p_new = jnp.exp(s_new - m)
    denom = jnp.sum(p_past, axis=-1, keepdims=True) + jnp.sum(p_new, axis=-1, keepdims=True)
    o = _dot(p_past.astype(BF16), kc) + _dot(p_new.astype(BF16), knew[:, :KV_LORA])
    o_ref[...] = (o / denom).astype(o_ref.dtype)


def _attn_sample(q_s, knew, cache_ckv, cache_kpe_t, page_table, j):
    nb, rows, _ = q_s.shape
    n_pages = page_table.shape[1]
    past = n_pages * PAGE_SIZE
    kernel = functools.partial(_attn_sample_kernel, layer=j, n_pages=n_pages)
    grid_spec = pltpu.PrefetchScalarGridSpec(
        num_scalar_prefetch=1,
        grid=(nb,),
        in_specs=[
            pl.BlockSpec((None, rows, QK_CAT), lambda b, pt: (b, 0, 0)),
            pl.BlockSpec((None, knew.shape[1], QK_CAT), lambda b, pt: (b, 0, 0)),
            pl.BlockSpec(memory_space=pl.ANY),
            pl.BlockSpec(memory_space=pl.ANY),
        ],
        out_specs=pl.BlockSpec((None, rows, KV_LORA), lambda b, pt: (b, 0, 0)),
        scratch_shapes=[
            pltpu.VMEM((2, past, KV_LORA), F32),
            pltpu.VMEM((2, QK_ROPE, past), F32),
            pltpu.SemaphoreType.DMA((2, 2)),
        ],
    )
    return pl.pallas_call(
        kernel,
        grid_spec=grid_spec,
        out_shape=jax.ShapeDtypeStruct((nb, rows, KV_LORA), BF16),
        compiler_params=pltpu.CompilerParams(dimension_semantics=("arbitrary",),
                                             vmem_limit_bytes=VMEM_LIMIT),
        name="attn_sample",
    )(page_table.reshape(-1), q_s, knew, cache_ckv, cache_kpe_t)


def _uv_kernel(o_ref, wuv_ref, a_ref):
    outs = [_dot(o_ref[:, hh * KV_LORA:(hh + 1) * KV_LORA], wuv_ref[hh])
            for hh in range(MLA_HEADS)]
    a_ref[...] = jnp.concatenate(outs, axis=-1).astype(a_ref.dtype)


def _uv_project(o_lat, prm, j):
    n = o_lat.shape[0]
    return pl.pallas_call(
        _uv_kernel,
        grid=(1,),
        in_specs=[_const_spec((n, MLA_HEADS * KV_LORA)),
                  _layer_spec((MLA_HEADS, KV_LORA, V_DIM), j)],
        out_specs=pl.BlockSpec((n, MLA_HEADS * V_DIM), lambda i: (0, 0)),
        out_shape=jax.ShapeDtypeStruct((n, MLA_HEADS * V_DIM), BF16),
        compiler_params=pltpu.CompilerParams(dimension_semantics=("arbitrary",),
                                             vmem_limit_bytes=VMEM_LIMIT),
        name="uv_project",
    )(o_lat, prm["w_uv"])


def _ffn_and_gate(h1, p_ref, gffn_ref, wg_ref, wu_ref, wd_ref, gpe_ref, wpg_ref, wpe_ref):
    hn = _rms(h1, gffn_ref[...]).astype(BF16)
    act = (jax.nn.silu(_dot(hn, wg_ref[...])) * _dot(hn, wu_ref[...])).astype(BF16)
    h2 = h1 + _dot(act, wd_ref[...])
    gate = jax.nn.sigmoid(_dot(_rms(h2, gpe_ref[...]).astype(BF16), wpg_ref[...]))
    return h2 + _dot(p_ref[...].astype(BF16), wpe_ref[...]) * gate


def _post_ab_kernel(h_ref, attn_ref, lru_ref, p_ref, wout_ref, gffn_ref, wg_ref, wu_ref, wd_ref,
                    gpe_ref, wpg_ref, wpe_ref, out_ref):
    n_attn = MLA_HEADS * V_DIM
    mix = _dot(attn_ref[...], wout_ref[:n_attn, :]) + _dot(lru_ref[...], wout_ref[n_attn:, :])
    h1 = h_ref[...] + mix
    out_ref[...] = _ffn_and_gate(h1, p_ref, gffn_ref, wg_ref, wu_ref, wd_ref, gpe_ref, wpg_ref,
                                 wpe_ref)


def _post_c_kernel(h_ref, p_ref, gmix_ref, winc_ref, lng_ref, lnb_ref, ws_ref, bs_ref, woutc_ref,
                   gffn_ref, wg_ref, wu_ref, wd_ref, gpe_ref, wpg_ref, wpe_ref, gfin_ref,
                   out_ref, *rest, block_len, final):
    sp_sc = rest[-1]
    h = h_ref[...]
    tm = h.shape[0]
    z = jax.nn.gelu(_dot(_rms(h, gmix_ref[...]).astype(BF16), winc_ref[...]))
    u = z[:, :GMLP_WIDTH]
    v = z[:, GMLP_WIDTH:]
    vc = v - jnp.mean(v, axis=-1, keepdims=True)
    v = vc * lax.rsqrt(jnp.mean(vc * vc, axis=-1, keepdims=True) + EPS) * lng_ref[...] + lnb_ref[...]
    if len(rest) == 2:
        rest[0][...] = v
    vb = v.astype(BF16)
    t_idx = lax.broadcasted_iota(jnp.int32, (CHUNK, CHUNK), 0)
    s_idx = lax.broadcasted_iota(jnp.int32, (CHUNK, CHUNK), 1)
    mask = (s_idx <= t_idx) & ((s_idx // block_len) == (t_idx // block_len))
    for gg in range(GMLP_GROUPS):
        wm = jnp.where(mask, ws_ref[gg], 0.0).astype(BF16)
        cols = slice(gg * GMLP_GROUP_DIM, (gg + 1) * GMLP_GROUP_DIM)
        for cc in range(tm // CHUNK):
            rows = slice(cc * CHUNK, (cc + 1) * CHUNK)
            sp_sc[rows, cols] = _dot(wm, vb[rows, cols]) + bs_ref[:, cols]
    mix = _dot((u * sp_sc[...]).astype(BF16), woutc_ref[...])
    h3 = _ffn_and_gate(h + mix, p_ref, gffn_ref, wg_ref, wu_ref, wd_ref, gpe_ref, wpg_ref, wpe_ref)
    out_ref[...] = _rms(h3, gfin_ref[...]) if final else h3


def _ffn_specs(i):
    return [
        _layer_spec((1, D_MODEL), i),
        _layer_spec((D_MODEL, FFN_HIDDEN), i),
        _layer_spec((D_MODEL, FFN_HIDDEN), i),
        _layer_spec((FFN_HIDDEN, D_MODEL), i),
        _layer_spec((1, D_MODEL), i),
        _layer_spec((D_MODEL, D_MODEL), i),
        _layer_spec((PLE_DIM, D_MODEL), i),
    ]


def _ffn_args(prm):
    return (prm["g_ffn"], prm["w_gate"], prm["w_up"], prm["w_down"], prm["g_pe"], prm["w_pg"],
            prm["w_pe"])


def _post_ab(h, attn, lru_y, p, prm, i):
    n = h.shape[0]
    tm = ROW_TILE
    j = i // 2
    row = lambda w: pl.BlockSpec((tm, w), lambda r: (r, 0))
    return pl.pallas_call(
        _post_ab_kernel,
        grid=(n // tm,),
        in_specs=[
            row(D_MODEL), row(MLA_HEADS * V_DIM), row(LRU_WIDTH),
            pl.BlockSpec((None, tm, PLE_DIM), lambda r: (i, r, 0)),
            _layer_spec((MLA_HEADS * V_DIM + LRU_WIDTH, D_MODEL), j),
        ] + _ffn_specs(i),
        out_specs=row(D_MODEL),
        out_shape=jax.ShapeDtypeStruct((n, D_MODEL), F32),
        compiler_params=pltpu.CompilerParams(dimension_semantics=("parallel",),
                                             vmem_limit_bytes=VMEM_LIMIT),
        name="post_ab",
    )(h, attn, lru_y, p, prm["w_out_ab"], *_ffn_args(prm))


def _post_c(h, p, prm, i, ws_tiled, bs_full, block_len, emit_v):
    n = h.shape[0]
    tm = ROW_TILE
    j = i // 2
    final = i == DEPTH - 1
    row = lambda w: pl.BlockSpec((tm, w), lambda r: (r, 0))
    kernel = functools.partial(_post_c_kernel, block_len=block_len, final=final)
    out_specs = [row(D_MODEL)] + ([row(GMLP_WIDTH)] if emit_v else [])
    out_shape = [jax.ShapeDtypeStruct((n, D_MODEL), F32)]
    if emit_v:
        out_shape.append(jax.ShapeDtypeStruct((n, GMLP_WIDTH), F32))
    outs = pl.pallas_call(
        kernel,
        grid=(n // tm,),
        in_specs=[
            row(D_MODEL),
            pl.BlockSpec((None, tm, PLE_DIM), lambda r: (i, r, 0)),
            _layer_spec((1, D_MODEL), i),
            _layer_spec((D_MODEL, 2 * GMLP_WIDTH), j),
            _layer_spec((1, GMLP_WIDTH), j),
            _layer_spec((1, GMLP_WIDTH), j),
            _layer_spec((GMLP_GROUPS, CHUNK, CHUNK), j),
            _layer_spec((CHUNK, GMLP_WIDTH), j),
            _layer_spec((GMLP_WIDTH, D_MODEL), j),
        ] + _ffn_specs(i) + [_const_spec((1, D_MODEL))],
        out_specs=out_specs,
        out_shape=out_shape,
        scratch_shapes=[pltpu.VMEM((tm, GMLP_WIDTH), F32)],
        compiler_params=pltpu.CompilerParams(dimension_semantics=("parallel",),
                                             vmem_limit_bytes=VMEM_LIMIT),
        name="post_c",
    )(h, p, prm["g_mix"], prm["w_in_c"], prm["ln_g_c"], prm["ln_b_c"], ws_tiled, bs_full,
      prm["w_out_c"], *_ffn_args(prm), prm["g_final"])
    return outs[0], (outs[1] if emit_v else None)


def _rope_tables(pos, reps=1):
    inv = jnp.exp(-math.log(ROPE_THETA) * jnp.arange(ROPE_HALF, dtype=F32) / ROPE_HALF)
    ang = pos.astype(F32)[:, None] * inv[None, :]
    cos, sin = jnp.cos(ang), jnp.sin(ang)
    t = pos.shape[0]
    head = jnp.zeros((t, QK_NOPE), F32)
    tail = jnp.zeros((t, LANES - QK_NOPE - QK_ROPE), F32)
    csk = jnp.concatenate([head, cos, cos, tail], axis=1)
    snk = jnp.concatenate([head, -sin, sin, tail], axis=1)
    log2_scale = ATTN_SCALE * math.log2(math.e)
    csq = jnp.concatenate([head + 1.0, cos, cos, tail], axis=1) * log2_scale
    tables = {"csk": csk, "snk": snk, "csq": csq, "snq": snk * log2_scale}
    return {name: jnp.tile(tab, (reps, 1)) for name, tab in tables.items()}


def _prepare_params(g_mix, g_ffn, g_pe, g_final, w_in_ab, g_qnorm, g_kvnorm, w_uq, w_uk, w_uv,
                    conv_w, conv_b, w_rg, b_rg, w_ig, b_ig, lru_lambda, w_out_ab, w_in_c, ln_g_c,
                    ln_b_c, w_s, b_s, w_out_c, w_gate, w_up, w_down, w_pe, w_pg):
    n_ab = w_in_ab.shape[0]
    o1 = Q_LORA + KV_LORA
    o2 = o1 + QK_ROPE
    pad_tail = LANES - QK_NOPE - QK_ROPE
    w_in = jnp.concatenate(
        [w_in_ab[:, :, :o1], w_in_ab[:, :, o2:], jnp.zeros((n_ab, D_MODEL, QK_NOPE), F32),
         w_in_ab[:, :, o1:o2], jnp.zeros((n_ab, D_MODEL, pad_tail), F32)], axis=2)
    uq = w_uq.reshape(n_ab, Q_LORA, MLA_HEADS, QK_NOPE + QK_ROPE)
    uq = jnp.pad(uq, ((0, 0), (0, 0), (0, 0), (0, pad_tail))).reshape(n_ab, Q_LORA, MLA_HEADS * LANES)
    uk_t = jnp.pad(w_uk.transpose(0, 3, 1, 2), ((0, 0), (0, 0), (0, 0), (0, LANES - QK_NOPE)))
    uk_t = uk_t.reshape(n_ab, KV_LORA, MLA_HEADS * LANES)
    uv_cat = w_uv.transpose(0, 2, 1, 3).reshape(n_ab, KV_LORA, MLA_HEADS * V_DIM)

    def block_diag(w):
        eye = jnp.eye(LRU_HEADS, dtype=F32)
        return jnp.einsum("jhab,hg->jhagb", w, eye).reshape(n_ab, LRU_WIDTH, LRU_WIDTH)

    vec = lambda a: a[:, None, :]
    return {
        "g_mix": vec(g_mix), "g_ffn": vec(g_ffn), "g_pe": vec(g_pe), "g_final": g_final[None, :],
        "w_in_ab": w_in.astype(BF16), "g_qnorm": vec(g_qnorm), "g_kvnorm": vec(g_kvnorm),
        "w_uq": uq.astype(BF16),
        "w_uk": w_uk.astype(BF16), "w_uv": w_uv.astype(BF16),
        "w_uk_t": uk_t.astype(BF16), "w_uv_cat": uv_cat.astype(BF16),
        "conv_w": conv_w, "conv_b": vec(conv_b),
        "w_gate_lru": jnp.concatenate([block_diag(w_rg), block_diag(w_ig)], axis=2).astype(BF16),
        "b_rg": vec(b_rg), "b_ig": vec(b_ig), "lru_lambda": vec(lru_lambda),
        "w_out_ab": w_out_ab.astype(BF16),
        "w_in_c": w_in_c.astype(BF16), "ln_g_c": vec(ln_g_c), "ln_b_c": vec(ln_b_c),
        "w_s": w_s, "b_s": b_s, "w_out_c": w_out_c.astype(BF16),
        "w_gate": w_gate.astype(BF16), "w_up": w_up.astype(BF16), "w_down": w_down.astype(BF16),
        "w_pe": w_pe.astype(BF16), "w_pg": w_pg.astype(BF16),
    }


def _gmlp_spatial(prm, block_len):
    reps = CHUNK // block_len
    ws = jnp.tile(prm["w_s"][:, :, :block_len, :block_len], (1, 1, reps, reps))
    bs = jnp.tile(prm["b_s"][:, :, :block_len], (1, 1, reps))
    bs = jnp.repeat(jnp.swapaxes(bs, 1, 2), GMLP_GROUP_DIM, axis=2)
    return ws, bs


def _trunk(x, p, tables, n_pos_tiles, prm, past, batch, seq):
    n = batch * seq
    h = x.reshape(n, D_MODEL)
    p = p.reshape(DEPTH, n, PLE_DIM)
    block_len = min(seq, CHUNK)
    ws_tiled, bs_full = _gmlp_spatial(prm, block_len)
    ckv, kpe, lru, conv, vrows = [], [], [], [], []
    for i in range(DEPTH):
        j = i // 2
        if i % 2 == 0:
            if past is None:
                q, k, v, c_kv, k_pe, zx, zg = _in_even(h, tables, prm, j, n_pos_tiles, False)
                attn = _attn_prompt(q, k, v, batch, seq)
                lru_y, h_last, buf = _lru_prompt(zx, zg, prm, j, batch, seq)
                h_last = h_last.reshape(batch, LRU_WIDTH)
            else:
                qcat, kcat, c_kv, k_pe, zx, zg = _in_even(h, tables, prm, j, n_pos_tiles, True)
                q_s = qcat.reshape(MLA_HEADS, batch, seq, QK_CAT).transpose(1, 2, 0, 3)
                q_s = q_s.reshape(batch, seq * MLA_HEADS, QK_CAT)
                knew = jnp.pad(kcat.reshape(batch, seq, QK_CAT),
                               ((0, 0), (0, 2 * SUBLANES - seq), (0, 0)))
                o_lat = _attn_sample(q_s, knew, past["cache_ckv"], past["cache_kpe_t"],
                                     past["page_table"], j)
                attn = _uv_project(o_lat.reshape(n, MLA_HEADS * KV_LORA), prm, j)
                to_t = lambda a: a.reshape(batch, seq, LRU_WIDTH).transpose(1, 0, 2)
                y_t, h_last = _lru_sample(to_t(zx), to_t(zg),
                                          past["state_conv"][j].transpose(1, 0, 2),
                                          past["state_lru"][j], prm, j)
                lru_y = y_t.transpose(1, 0, 2).reshape(n, LRU_WIDTH)
                buf = jnp.concatenate([past["state_conv"][j], zx.reshape(batch, seq, LRU_WIDTH)],
                                      axis=1)[:, seq:]
            h = _post_ab(h, attn, lru_y, p, prm, i)
            ckv.append(c_kv.reshape(batch, seq, KV_LORA))
            kpe.append(k_pe.reshape(batch, seq, QK_ROPE))
            lru.append(h_last)
            conv.append(buf)
        else:
            h, v = _post_c(h, p, prm, i, ws_tiled, bs_full, block_len, past is not None)
            if v is not None:
                vrows.append(v.reshape(batch, seq, GMLP_WIDTH))
    return h.reshape(batch, seq, D_MODEL), ckv, kpe, lru, conv, vrows


def kernel(x_prompt, x_sample, cache_ckv, cache_kpe, state_lru, state_conv, page_table, p_prompt, p_sample, g_mix, g_ffn, g_pe, g_final, w_in_ab, g_qnorm, g_kvnorm, w_uq, w_uk, w_uv, conv_w, conv_b, w_rg, b_rg, w_ig, b_ig, lru_lambda, w_out_ab, w_in_c, ln_g_c, ln_b_c, w_s, b_s, w_out_c, w_gate, w_up, w_down, w_pe, w_pg):
    prm = _prepare_params(g_mix, g_ffn, g_pe, g_final, w_in_ab, g_qnorm, g_kvnorm, w_uq, w_uk,
                          w_uv, conv_w, conv_b, w_rg, b_rg, w_ig, b_ig, lru_lambda, w_out_ab,
                          w_in_c, ln_g_c, ln_b_c, w_s, b_s, w_out_c, w_gate, w_up, w_down, w_pe,
                          w_pg)
    batch, seq, _ = x_prompt.shape
    dec_batch, dec_seq, _ = x_sample.shape
    past_len = page_table.shape[1] * PAGE_SIZE

    tables_p = _rope_tables(jnp.arange(seq, dtype=jnp.int32))
    y_prompt, ckv_p, kpe_p, lru_p, conv_p, _ = _trunk(
        x_prompt, p_prompt, tables_p, seq // ROW_TILE, prm, None, batch, seq)

    tables_s = _rope_tables(past_len + jnp.arange(dec_seq, dtype=jnp.int32), ROW_TILE // dec_seq)
    past = {"cache_ckv": cache_ckv, "cache_kpe_t": jnp.swapaxes(cache_kpe, 2, 3), "state_lru": state_lru,
            "state_conv": state_conv, "page_table": page_table}
    y_sample, ckv_s, kpe_s, lru_s, conv_s, v_s = _trunk(
        x_sample, p_sample, tables_s, 1, prm, past, dec_batch, dec_seq)

    return (y_prompt, y_sample,
            jnp.stack(ckv_p), jnp.stack(kpe_p), jnp.stack(lru_p), jnp.stack(conv_p),
            jnp.stack(ckv_s), jnp.stack(kpe_s), jnp.stack(lru_s), jnp.stack(conv_s),
            jnp.stack(v_s))
```

```python
import functools
import math

import jax
import jax.numpy as jnp
from jax import lax
from jax.experimental import pallas as pl
from jax.experimental.pallas import tpu as pltpu

F32 = jnp.float32
BF16 = jnp.bfloat16

D_MODEL = 1024
DEPTH = 4
PAGE_SIZE = 128
MLA_HEADS = 8
Q_LORA = 384
KV_LORA = 256
QK_NOPE = 64
QK_ROPE = 32
ROPE_HALF = QK_ROPE // 2
V_DIM = 64
ROPE_THETA = 10000.0
ATTN_SCALE = 1.0 / math.sqrt(QK_NOPE + QK_ROPE)
LRU_WIDTH = 512
LRU_HEADS = 8
LRU_HEAD_DIM = LRU_WIDTH // LRU_HEADS
CONV_WIDTH = 4
LRU_C = 8.0
CHUNK = 128
GMLP_WIDTH = 1024
GMLP_GROUPS = 8
GMLP_GROUP_DIM = GMLP_WIDTH // GMLP_GROUPS
FFN_HIDDEN = -(-8 * D_MODEL // (3 * 256)) * 256
PLE_DIM = 256
EPS = 1e-6

LANES = 128
SUBLANES = 8
QK_CAT = KV_LORA + LANES
IN_AB_PAD = Q_LORA + KV_LORA + 2 * LRU_WIDTH + LANES
VMEM_LIMIT = 56 * 1024 * 1024

ROW_TILE = 256
ATTN_TILE = 256
LRU_TILE = 256
SAMPLE_KEY_SPANS = 4


def _const_spec(shape):
    nd = len(shape)
    return pl.BlockSpec(shape, lambda *_: (0,) * nd, pipeline_mode=pl.Buffered(1))


def _layer_spec(shape, layer):
    nd = len(shape)
    return pl.BlockSpec((None,) + tuple(shape), lambda *_: (layer,) + (0,) * nd,
                        pipeline_mode=pl.Buffered(1))


def _rms(x, g):
    return x * lax.rsqrt(jnp.mean(x * x, axis=-1, keepdims=True) + EPS) * g


def _dot(a, b):
    return jnp.dot(a, b, preferred_element_type=F32)


def _dot_nt(a, b):
    return lax.dot_general(a, b, (((1,), (1,)), ((), ())), preferred_element_type=F32)


def _rope_block(blk, cs, sn):
    lane = lax.broadcasted_iota(jnp.int32, blk.shape, 1)
    swapped = jnp.where(lane < QK_NOPE + ROPE_HALF,
                        pltpu.roll(blk, LANES - ROPE_HALF, 1),
                        pltpu.roll(blk, ROPE_HALF, 1))
    return blk * cs + swapped * sn


def _in_even_common(h_ref, csk_ref, snk_ref, gmix_ref, win_ref, gq_ref, wuq_ref, gkv_ref,
                    ckv_ref, kpe_ref, zx_ref, zg_ref):
    xn = _rms(h_ref[...], gmix_ref[...]).astype(BF16)
    z = _dot(xn, win_ref[...])
    o_kv = Q_LORA
    o_x = o_kv + KV_LORA
    o_g = o_x + LRU_WIDTH
    o_pe = o_g + LRU_WIDTH
    zx_ref[...] = z[:, o_x:o_g]
    zg_ref[...] = z[:, o_g:o_pe]
    ckv = _rms(z[:, o_kv:o_x], gkv_ref[...])
    kpe_blk = _rope_block(z[:, o_pe:o_pe + LANES], csk_ref[...], snk_ref[...])
    ckv_ref[...] = ckv
    kpe_ref[...] = kpe_blk[:, QK_NOPE:QK_NOPE + QK_ROPE]
    qn = _rms(z[:, :Q_LORA], gq_ref[...]).astype(BF16)
    q = _dot(qn, wuq_ref[...])
    return ckv, kpe_blk, q


def _in_even_sample_kernel(h_ref, csk_ref, snk_ref, gmix_ref, win_ref, gq_ref, wuq_ref, gkv_ref,
                           wuk_ref, qcat_ref, kcat_ref, ckv_ref, kpe_ref, zx_ref, zg_ref):
    ckv, kpe_blk, q = _in_even_common(h_ref, csk_ref, snk_ref, gmix_ref, win_ref, gq_ref, wuq_ref,
                                      gkv_ref, ckv_ref, kpe_ref, zx_ref, zg_ref)
    kcat_ref[:, :KV_LORA] = ckv.astype(BF16)
    kcat_ref[:, KV_LORA:] = kpe_blk.astype(BF16)
    for hh in range(MLA_HEADS):
        blk = q[:, hh * LANES:(hh + 1) * LANES]
        qcat_ref[hh, :, :KV_LORA] = _dot(blk[:, :QK_NOPE].astype(BF16), wuk_ref[hh]).astype(BF16)
        qcat_ref[hh, :, KV_LORA:] = _rope_block(blk, csk_ref[...], snk_ref[...]).astype(BF16)


def _in_even_prompt_kernel(h_ref, csk_ref, snk_ref, gmix_ref, win_ref, gq_ref, wuq_ref, gkv_ref,
                           csq_ref, snq_ref, wukt_ref, wuvc_ref,
                           q_ref, k_ref, v_ref, ckv_ref, kpe_ref, zx_ref, zg_ref):
    ckv, kpe_blk, q = _in_even_common(h_ref, csk_ref, snk_ref, gmix_ref, win_ref, gq_ref, wuq_ref,
                                      gkv_ref, ckv_ref, kpe_ref, zx_ref, zg_ref)
    cb = ckv.astype(BF16)
    k_nope = _dot(cb, wukt_ref[...])
    v = _dot(cb, wuvc_ref[...])
    ones_lane = (lax.broadcasted_iota(jnp.int32, (1, LANES), 1) == V_DIM).astype(F32)
    for hh in range(MLA_HEADS):
        blk = slice(hh * LANES, (hh + 1) * LANES)
        q_ref[hh] = _rope_block(q[:, blk], csq_ref[...], snq_ref[...]).astype(BF16)
        k_ref[hh] = (k_nope[:, blk] + kpe_blk).astype(BF16)
        v_ref[hh] = (v[:, blk] + ones_lane).astype(BF16)


def _in_even(h, tables, prm, j, n_pos_tiles, absorbed):
    n = h.shape[0]
    tm = ROW_TILE
    row = lambda w: pl.BlockSpec((tm, w), lambda r: (r, 0))
    heads = lambda w: pl.BlockSpec((MLA_HEADS, tm, w), lambda r: (0, r, 0))
    pos = pl.BlockSpec((tm, LANES), lambda r: (r % n_pos_tiles, 0))
    in_specs = [
        row(D_MODEL), pos, pos,
        _layer_spec((1, D_MODEL), 2 * j),
        _layer_spec((D_MODEL, IN_AB_PAD), j),
        _layer_spec((1, Q_LORA), j),
        _layer_spec((Q_LORA, MLA_HEADS * LANES), j),
        _layer_spec((1, KV_LORA), j),
    ]
    args = [h, tables["csk"], tables["snk"], prm["g_mix"], prm["w_in_ab"], prm["g_qnorm"],
            prm["w_uq"], prm["g_kvnorm"]]
    tail_specs = (row(KV_LORA), row(QK_ROPE), row(LRU_WIDTH), row(LRU_WIDTH))
    tail_shapes = (
        jax.ShapeDtypeStruct((n, KV_LORA), F32),
        jax.ShapeDtypeStruct((n, QK_ROPE), F32),
        jax.ShapeDtypeStruct((n, LRU_WIDTH), F32),
        jax.ShapeDtypeStruct((n, LRU_WIDTH), F32),
    )
    if absorbed:
        body = _in_even_sample_kernel
        in_specs += [_layer_spec((MLA_HEADS, QK_NOPE, KV_LORA), j)]
        args += [prm["w_uk"]]
        out_specs = (heads(QK_CAT), row(QK_CAT)) + tail_specs
        out_shape = (jax.ShapeDtypeStruct((MLA_HEADS, n, QK_CAT), BF16),
                     jax.ShapeDtypeStruct((n, QK_CAT), BF16)) + tail_shapes
    else:
        body = _in_even_prompt_kernel
        in_specs += [pos, pos,
                     _layer_spec((KV_LORA, MLA_HEADS * LANES), j),
                     _layer_spec((KV_LORA, MLA_HEADS * LANES), j)]
        args += [tables["csq"], tables["snq"], prm["w_uk_t"], prm["w_uv_cat"]]
        out_specs = (heads(LANES), heads(LANES), heads(LANES)) + tail_specs
        out_shape = (jax.ShapeDtypeStruct((MLA_HEADS, n, LANES), BF16),) * 3 + tail_shapes
    return pl.pallas_call(
        body,
        grid=(n // tm,),
        in_specs=in_specs,
        out_specs=out_specs,
        out_shape=out_shape,
        compiler_params=pltpu.CompilerParams(dimension_semantics=("parallel",),
                                             vmem_limit_bytes=VMEM_LIMIT),
        name="in_even_absorbed" if absorbed else "in_even_heads",
    )(*args)


def _lru_gates(xc, wgate_ref, brg_ref, big_ref, lam_ref):
    g = _dot(xc.astype(BF16), wgate_ref[...])
    r = jax.nn.sigmoid(g[:, :LRU_WIDTH] + brg_ref[...])
    ig = jax.nn.sigmoid(g[:, LRU_WIDTH:] + big_ref[...])
    x = -lam_ref[...]
    softplus = jnp.maximum(x, 0.0) + jnp.log1p(jnp.exp(-jnp.abs(x)))
    log_a = -LRU_C * r * softplus
    a = jnp.exp(log_a)
    th = jnp.tanh(log_a)
    bx = jnp.sqrt(-2.0 * th / (1.0 - th)) * ig * xc
    return a, bx


def _shift_rows(x, s, fill):
    return jnp.concatenate([jnp.full((s, x.shape[1]), fill, x.dtype), x[:-s]], axis=0)


def _lru_prompt_kernel(zx_ref, zg_ref, cw_ref, cb_ref, wgate_ref, brg_ref, big_ref, lam_ref,
                       y_ref, hlast_ref, conv_ref, xbuf, hprev):
    t = pl.program_id(1)
    tl = zx_ref.shape[0]
    pad = SUBLANES

    @pl.when(t == 0)
    def _():
        xbuf[:pad, :] = jnp.zeros((pad, LRU_WIDTH), F32)
        hprev[...] = jnp.zeros_like(hprev)

    x = zx_ref[...]
    xbuf[pad:, :] = x
    xc = cb_ref[...] + xbuf[pad - 3:pad - 3 + tl, :] * cw_ref[0:1, :]
    xc = xc + xbuf[pad - 2:pad - 2 + tl, :] * cw_ref[1:2, :]
    xc = xc + xbuf[pad - 1:pad - 1 + tl, :] * cw_ref[2:3, :]
    xc = xc + x * cw_ref[3:4, :]
    xbuf[:pad, :] = x[tl - pad:, :]

    a, b = _lru_gates(xc, wgate_ref, brg_ref, big_ref, lam_ref)
    s = 1
    while s < tl:
        b = a * _shift_rows(b, s, 0.0) + b
        a = a * _shift_rows(a, s, 1.0)
        s *= 2
    hs = a * hprev[...] + b
    hprev[...] = hs[tl - 1:, :]
    y_ref[...] = (hs * jax.nn.gelu(zg_ref[...])).astype(y_ref.dtype)

    @pl.when(t == pl.num_programs(1) - 1)
    def _():
        hlast_ref[...] = hs[tl - 1:, :]
        conv_ref[...] = x[tl - (CONV_WIDTH - 1):, :]


def _lru_prompt(zx, zg, prm, j, batch, seq):
    tl = LRU_TILE
    nt = seq // tl
    row = pl.BlockSpec((tl, LRU_WIDTH), lambda b, t: (b * nt + t, 0))
    return pl.pallas_call(
        _lru_prompt_kernel,
        grid=(batch, nt),
        in_specs=[
            row, row,
            _layer_spec((CONV_WIDTH, LRU_WIDTH), j),
            _layer_spec((1, LRU_WIDTH), j),
            _layer_spec((LRU_WIDTH, 2 * LRU_WIDTH), j),
            _layer_spec((1, LRU_WIDTH), j),
            _layer_spec((1, LRU_WIDTH), j),
            _layer_spec((1, LRU_WIDTH), j),
        ],
        out_specs=(
            row,
            pl.BlockSpec((None, 1, LRU_WIDTH), lambda b, t: (b, 0, 0)),
            pl.BlockSpec((None, CONV_WIDTH - 1, LRU_WIDTH), lambda b, t: (b, 0, 0)),
        ),
        out_shape=(
            jax.ShapeDtypeStruct((batch * seq, LRU_WIDTH), BF16),
            jax.ShapeDtypeStruct((batch, 1, LRU_WIDTH), F32),
            jax.ShapeDtypeStruct((batch, CONV_WIDTH - 1, LRU_WIDTH), F32),
        ),
        scratch_shapes=[pltpu.VMEM((SUBLANES + tl, LRU_WIDTH), F32),
                        pltpu.VMEM((1, LRU_WIDTH), F32)],
        compiler_params=pltpu.CompilerParams(dimension_semantics=("parallel", "arbitrary"),
                                             vmem_limit_bytes=VMEM_LIMIT),
        name="lru_prompt",
    )(zx, zg, prm["conv_w"], prm["conv_b"], prm["w_gate_lru"], prm["b_rg"], prm["b_ig"],
      prm["lru_lambda"])


def _lru_sample_kernel(zx_ref, zg_ref, buf_ref, h0_ref, cw_ref, cb_ref, wgate_ref, brg_ref,
                       big_ref, lam_ref, y_ref, hlast_ref):
    nt = zx_ref.shape[0]
    nb = zx_ref.shape[1]
    xpad = [buf_ref[k] for k in range(CONV_WIDTH - 1)] + [zx_ref[k] for k in range(nt)]
    xcs = []
    for t in range(nt):
        xc = cb_ref[...] + xpad[t] * cw_ref[0:1, :]
        for k in range(1, CONV_WIDTH):
            xc = xc + xpad[t + k] * cw_ref[k:k + 1, :]
        xcs.append(xc)
    a, b = _lru_gates(jnp.concatenate(xcs, axis=0), wgate_ref, brg_ref, big_ref, lam_ref)
    h = h0_ref[...]
    for t in range(nt):
        h = a[t * nb:(t + 1) * nb] * h + b[t * nb:(t + 1) * nb]
        y_ref[t] = (h * jax.nn.gelu(zg_ref[t])).astype(y_ref.dtype)
    hlast_ref[...] = h


def _lru_sample(zx_t, zg_t, buf_t, h0, prm, j):
    nt, nb, _ = zx_t.shape
    return pl.pallas_call(
        _lru_sample_kernel,
        grid=(1,),
        in_specs=[
            _const_spec((nt, nb, LRU_WIDTH)), _const_spec((nt, nb, LRU_WIDTH)),
            _const_spec((CONV_WIDTH - 1, nb, LRU_WIDTH)), _const_spec((nb, LRU_WIDTH)),
            _layer_spec((CONV_WIDTH, LRU_WIDTH), j),
            _layer_spec((1, LRU_WIDTH), j),
            _layer_spec((LRU_WIDTH, 2 * LRU_WIDTH), j),
            _layer_spec((1, LRU_WIDTH), j),
            _layer_spec((1, LRU_WIDTH), j),
            _layer_spec((1, LRU_WIDTH), j),
        ],
        out_specs=(
            pl.BlockSpec((nt, nb, LRU_WIDTH), lambda i: (0, 0, 0)),
            pl.BlockSpec((nb, LRU_WIDTH), lambda i: (0, 0)),
        ),
        out_shape=(
            jax.ShapeDtypeStruct((nt, nb, LRU_WIDTH), BF16),
            jax.ShapeDtypeStruct((nb, LRU_WIDTH), F32),
        ),
        compiler_params=pltpu.CompilerParams(dimension_semantics=("arbitrary",),
                                             vmem_limit_bytes=VMEM_LIMIT),
        name="lru_sample",
    )(zx_t, zg_t, buf_t, h0, prm["conv_w"], prm["conv_b"], prm["w_gate_lru"], prm["b_rg"],
      prm["b_ig"], prm["lru_lambda"])


def _attn_prompt_kernel(q_ref, k_ref, v_ref, o_ref, m_sc, acc_sc):
    i = pl.program_id(1)
    tq = q_ref.shape[1]
    m_sc[...] = jnp.full(m_sc.shape, -jnp.inf, F32)
    acc_sc[...] = jnp.zeros(acc_sc.shape, F32)

    def key_block(start, width, masked):
        keys = pl.ds(start, width)
        if masked:
            q_pos = i * tq + lax.broadcasted_iota(jnp.int32, (tq, width), 0)
            causal = start + lax.broadcasted_iota(jnp.int32, (tq, width), 1) <= q_pos
        scores = [_dot_nt(q_ref[hh], k_ref[hh, keys, :]) for hh in range(MLA_HEADS)]
        for hh in range(MLA_HEADS):
            s = scores[hh]
            if masked:
                s = jnp.where(causal, s, -jnp.inf)
            m_prev = m_sc[hh]
            m_new = jnp.maximum(m_prev, jnp.max(s, axis=-1, keepdims=True))
            alpha = jnp.exp2(m_prev - m_new)
            p = jnp.exp2(s - jnp.concatenate([m_new] * (width // LANES), axis=1))
            acc_sc[hh] = alpha * acc_sc[hh] + _dot(p.astype(BF16), v_ref[hh, keys, :])
            m_sc[hh] = m_new

    def full_pair(jb, carry):
        key_block(pl.multiple_of(jb * (2 * tq), 2 * tq), 2 * tq, False)
        return carry

    lax.fori_loop(0, i // 2, full_pair, 0)

    @pl.when(i % 2 == 1)
    def _():
        key_block(pl.multiple_of((i - 1) * tq, 2 * tq), 2 * tq, True)

    @pl.when(i % 2 == 0)
    def _():
        key_block(pl.multiple_of(i * tq, tq), tq, True)

    outs = []
    for hh in range(MLA_HEADS):
        acc = acc_sc[hh]
        outs.append(acc[:, :V_DIM] / acc[:, V_DIM:V_DIM + 1])
    o_ref[...] = jnp.concatenate(outs, axis=-1).astype(o_ref.dtype)


def _attn_prompt(q, k, v, batch, seq):
    tq = ATTN_TILE
    nq = seq // tq
    return pl.pallas_call(
        _attn_prompt_kernel,
        grid=(batch, nq),
        in_specs=[
            pl.BlockSpec((MLA_HEADS, tq, LANES), lambda b, i: (0, b * nq + i, 0)),
            pl.BlockSpec((MLA_HEADS, seq, LANES), lambda b, i: (0, b, 0)),
            pl.BlockSpec((MLA_HEADS, seq, LANES), lambda b, i: (0, b, 0)),
        ],
        out_specs=pl.BlockSpec((tq, MLA_HEADS * V_DIM), lambda b, i: (b * nq + i, 0)),
        out_shape=jax.ShapeDtypeStruct((batch * seq, MLA_HEADS * V_DIM), BF16),
        scratch_shapes=[pltpu.VMEM((MLA_HEADS, tq, LANES), F32),
                        pltpu.VMEM((MLA_HEADS, tq, LANES), F32)],
        compiler_params=pltpu.CompilerParams(dimension_semantics=("parallel", "arbitrary"),
                                             vmem_limit_bytes=VMEM_LIMIT),
        name="attn_prompt",
    )(q, k, v)


def _attn_sample_kernel(pt_ref, q_ref, knew_ref, ckv_hbm, kpe_hbm, o_ref, cbuf, pbuf, sems, *,
                        layer, n_pages):
    b = pl.program_id(0)
    nb = pl.num_programs(0)
    slot = b % 2

    def page_copies(bb, sl):
        copies = []
        for pg in range(n_pages):
            page = pt_ref[bb * n_pages + pg]
            rows = pl.ds(pg * PAGE_SIZE, PAGE_SIZE)
            copies.append(pltpu.make_async_copy(ckv_hbm.at[layer, page], cbuf.at[sl, rows],
                                                sems.at[0, sl]))
            copies.append(pltpu.make_async_copy(kpe_hbm.at[layer, page], pbuf.at[sl, :, rows],
                                                sems.at[1, sl]))
        return copies

    @pl.when(b == 0)
    def _():
        for cp in page_copies(b, slot):
            cp.start()

    @pl.when(b + 1 < nb)
    def _():
        for cp in page_copies(b + 1, 1 - slot):
            cp.start()

    for cp in page_copies(b, slot):
        cp.wait()

    q = q_ref[...]
    q_lat = q[:, :KV_LORA]
    o_pe = KV_LORA + QK_NOPE
    q_pe = q[:, o_pe:o_pe + QK_ROPE]
    knew = knew_ref[...]
    s_new = _dot_nt(q, knew) * ATTN_SCALE
    t_row = lax.broadcasted_iota(jnp.int32, s_new.shape, 0) // MLA_HEADS
    t_key = lax.broadcasted_iota(jnp.int32, s_new.shape, 1)
    s_new = jnp.where(t_key <= t_row, s_new, -jnp.inf)
    span = cbuf.shape[1] // SAMPLE_KEY_SPANS
    spans = [slice(c * span, (c + 1) * span) for c in range(SAMPLE_KEY_SPANS)]
    kcs = [cbuf[slot, sp, :].astype(BF16) for sp in spans]
    s_lat = [_dot_nt(q_lat, kc) for kc in kcs]
    s_pe = [_dot(q_pe, pbuf[slot, :, sp].astype(BF16)) for sp in spans]
    scores = [(a + b) * ATTN_SCALE for a, b in zip(s_lat, s_pe)]
    m = jnp.max(s_new, axis=-1, keepdims=True)
    for s in scores:
        m = jnp.maximum(m, jnp.max(s, axis=-1, keepdims=True))
    p_new = jnp.exp(s_new - m)
    denom = jnp.sum(p_new, axis=-1, keepdims=True)
    outs = [_dot(p_new.astype(BF16), knew[:, :KV_LORA])]
    for s, kc in zip(scores, kcs):
        p = jnp.exp(s - m)
        denom = denom + jnp.sum(p, axis=-1, keepdims=True)
        outs.append(_dot(p.astype(BF16), kc))
    while len(outs) > 1:
        outs = [outs[k] + outs[k + 1] if k + 1 < len(outs) else outs[k]
                for k in range(0, len(outs), 2)]
    o_ref[...] = (outs[0] / denom).astype(o_ref.dtype)


def _attn_sample(q_s, knew, cache_ckv, cache_kpe_t, page_table, j):
    nb, rows, _ = q_s.shape
    n_pages = page_table.shape[1]
    past = n_pages * PAGE_SIZE
    kernel = functools.partial(_attn_sample_kernel, layer=j, n_pages=n_pages)
    grid_spec = pltpu.PrefetchScalarGridSpec(
        num_scalar_prefetch=1,
        grid=(nb,),
        in_specs=[
            pl.BlockSpec((None, rows, QK_CAT), lambda b, pt: (b, 0, 0)),
            pl.BlockSpec((None, knew.shape[1], QK_CAT), lambda b, pt: (b, 0, 0)),
            pl.BlockSpec(memory_space=pl.ANY),
            pl.BlockSpec(memory_space=pl.ANY),
        ],
        out_specs=pl.BlockSpec((None, rows, KV_LORA), lambda b, pt: (b, 0, 0)),
        scratch_shapes=[
            pltpu.VMEM((2, past, KV_LORA), F32),
            pltpu.VMEM((2, QK_ROPE, past), F32),
            pltpu.SemaphoreType.DMA((2, 2)),
        ],
    )
    return pl.pallas_call(
        kernel,
        grid_spec=grid_spec,
        out_shape=jax.ShapeDtypeStruct((nb, rows, KV_LORA), BF16),
        compiler_params=pltpu.CompilerParams(dimension_semantics=("arbitrary",),
                                             vmem_limit_bytes=VMEM_LIMIT),
        name="attn_sample",
    )(page_table.reshape(-1), q_s, knew, cache_ckv, cache_kpe_t)


def _uv_kernel(o_ref, wuv_ref, a_ref):
    outs = [_dot(o_ref[:, hh * KV_LORA:(hh + 1) * KV_LORA], wuv_ref[hh])
            for hh in range(MLA_HEADS)]
    a_ref[...] = jnp.concatenate(outs, axis=-1).astype(a_ref.dtype)


def _uv_project(o_lat, prm, j):
    n = o_lat.shape[0]
    return pl.pallas_call(
        _uv_kernel,
        grid=(1,),
        in_specs=[_const_spec((n, MLA_HEADS * KV_LORA)),
                  _layer_spec((MLA_HEADS, KV_LORA, V_DIM), j)],
        out_specs=pl.BlockSpec((n, MLA_HEADS * V_DIM), lambda i: (0, 0)),
        out_shape=jax.ShapeDtypeStruct((n, MLA_HEADS * V_DIM), BF16),
        compiler_params=pltpu.CompilerParams(dimension_semantics=("arbitrary",),
                                             vmem_limit_bytes=VMEM_LIMIT),
        name="uv_project",
    )(o_lat, prm["w_uv"])


def _ffn_and_gate(h1, p_ref, gffn_ref, wg_ref, wu_ref, wd_ref, gpe_ref, wpg_ref, wpe_ref):
    hn = _rms(h1, gffn_ref[...]).astype(BF16)
    act = (jax.nn.silu(_dot(hn, wg_ref[...])) * _dot(hn, wu_ref[...])).astype(BF16)
    h2 = h1 + _dot(act, wd_ref[...])
    gate = jax.nn.sigmoid(_dot(_rms(h2, gpe_ref[...]).astype(BF16), wpg_ref[...]))
    return h2 + _dot(p_ref[...].astype(BF16), wpe_ref[...]) * gate


def _post_ab_kernel(h_ref, attn_ref, lru_ref, p_ref, wout_ref, gffn_ref, wg_ref, wu_ref, wd_ref,
                    gpe_ref, wpg_ref, wpe_ref, out_ref):
    n_attn = MLA_HEADS * V_DIM
    mix = _dot(attn_ref[...], wout_ref[:n_attn, :]) + _dot(lru_ref[...], wout_ref[n_attn:, :])
    h1 = h_ref[...] + mix
    out_ref[...] = _ffn_and_gate(h1, p_ref, gffn_ref, wg_ref, wu_ref, wd_ref, gpe_ref, wpg_ref,
                                 wpe_ref)


def _post_c_kernel(h_ref, p_ref, gmix_ref, winc_ref, lng_ref, lnb_ref, ws_ref, bs_ref, woutc_ref,
                   gffn_ref, wg_ref, wu_ref, wd_ref, gpe_ref, wpg_ref, wpe_ref, gfin_ref,
                   out_ref, *rest, block_len, final):
    sp_sc = rest[-1]
    h = h_ref[...]
    tm = h.shape[0]
    z = jax.nn.gelu(_dot(_rms(h, gmix_ref[...]).astype(BF16), winc_ref[...]))
    u = z[:, :GMLP_WIDTH]
    v = z[:, GMLP_WIDTH:]
    vc = v - jnp.mean(v, axis=-1, keepdims=True)
    v = vc * lax.rsqrt(jnp.mean(vc * vc, axis=-1, keepdims=True) + EPS) * lng_ref[...] + lnb_ref[...]
    if len(rest) == 2:
        rest[0][...] = v
    vb = v.astype(BF16)
    t_idx = lax.broadcasted_iota(jnp.int32, (CHUNK, CHUNK), 0)
    s_idx = lax.broadcasted_iota(jnp.int32, (CHUNK, CHUNK), 1)
    mask = (s_idx <= t_idx) & ((s_idx // block_len) == (t_idx // block_len))
    for gg in range(GMLP_GROUPS):
        wm = jnp.where(mask, ws_ref[gg], 0.0).astype(BF16)
        cols = slice(gg * GMLP_GROUP_DIM, (gg + 1) * GMLP_GROUP_DIM)
        for cc in range(tm // CHUNK):
            rows = slice(cc * CHUNK, (cc + 1) * CHUNK)
            sp_sc[rows, cols] = _dot(wm, vb[rows, cols]) + bs_ref[:, cols]
    mix = _dot((u * sp_sc[...]).astype(BF16), woutc_ref[...])
    h3 = _ffn_and_gate(h + mix, p_ref, gffn_ref, wg_ref, wu_ref, wd_ref, gpe_ref, wpg_ref, wpe_ref)
    out_ref[...] = _rms(h3, gfin_ref[...]) if final else h3


def _ffn_specs(i):
    return [
        _layer_spec((1, D_MODEL), i),
        _layer_spec((D_MODEL, FFN_HIDDEN), i),
        _layer_spec((D_MODEL, FFN_HIDDEN), i),
        _layer_spec((FFN_HIDDEN, D_MODEL), i),
        _layer_spec((1, D_MODEL), i),
        _layer_spec((D_MODEL, D_MODEL), i),
        _layer_spec((PLE_DIM, D_MODEL), i),
    ]


def _ffn_args(prm):
    return (prm["g_ffn"], prm["w_gate"], prm["w_up"], prm["w_down"], prm["g_pe"], prm["w_pg"],
            prm["w_pe"])


def _post_ab(h, attn, lru_y, p, prm, i):
    n = h.shape[0]
    tm = ROW_TILE
    j = i // 2
    row = lambda w: pl.BlockSpec((tm, w), lambda r: (r, 0))
    return pl.pallas_call(
        _post_ab_kernel,
        grid=(n // tm,),
        in_specs=[
            row(D_MODEL), row(MLA_HEADS * V_DIM), row(LRU_WIDTH),
            pl.BlockSpec((None, tm, PLE_DIM), lambda r: (i, r, 0)),
            _layer_spec((MLA_HEADS * V_DIM + LRU_WIDTH, D_MODEL), j),
        ] + _ffn_specs(i),
        out_specs=row(D_MODEL),
        out_shape=jax.ShapeDtypeStruct((n, D_MODEL), F32),
        compiler_params=pltpu.CompilerParams(dimension_semantics=("parallel",),
                                             vmem_limit_bytes=VMEM_LIMIT),
        name="post_ab",
    )(h, attn, lru_y, p, prm["w_out_ab"], *_ffn_args(prm))


def _post_c(h, p, prm, i, ws_tiled, bs_full, block_len, emit_v):
    n = h.shape[0]
    tm = ROW_TILE
    j = i // 2
    final = i == DEPTH - 1
    row = lambda w: pl.BlockSpec((tm, w), lambda r: (r, 0))
    kernel = functools.partial(_post_c_kernel, block_len=block_len, final=final)
    out_specs = [row(D_MODEL)] + ([row(GMLP_WIDTH)] if emit_v else [])
    out_shape = [jax.ShapeDtypeStruct((n, D_MODEL), F32)]
    if emit_v:
        out_shape.append(jax.ShapeDtypeStruct((n, GMLP_WIDTH), F32))
    outs = pl.pallas_call(
        kernel,
        grid=(n // tm,),
        in_specs=[
            row(D_MODEL),
            pl.BlockSpec((None, tm, PLE_DIM), lambda r: (i, r, 0)),
            _layer_spec((1, D_MODEL), i),
            _layer_spec((D_MODEL, 2 * GMLP_WIDTH), j),
            _layer_spec((1, GMLP_WIDTH), j),
            _layer_spec((1, GMLP_WIDTH), j),
            _layer_spec((GMLP_GROUPS, CHUNK, CHUNK), j),
            _layer_spec((CHUNK, GMLP_WIDTH), j),
            _layer_spec((GMLP_WIDTH, D_MODEL), j),
        ] + _ffn_specs(i) + [_const_spec((1, D_MODEL))],
        out_specs=out_specs,
        out_shape=out_shape,
        scratch_shapes=[pltpu.VMEM((tm, GMLP_WIDTH), F32)],
        compiler_params=pltpu.CompilerParams(dimension_semantics=("parallel",),
                                             vmem_limit_bytes=VMEM_LIMIT),
        name="post_c",
    )(h, p, prm["g_mix"], prm["w_in_c"], prm["ln_g_c"], prm["ln_b_c"], ws_tiled, bs_full,
      prm["w_out_c"], *_ffn_args(prm), prm["g_final"])
    return outs[0], (outs[1] if emit_v else None)


def _rope_tables(pos, reps=1):
    inv = jnp.exp(-math.log(ROPE_THETA) * jnp.arange(ROPE_HALF, dtype=F32) / ROPE_HALF)
    ang = pos.astype(F32)[:, None] * inv[None, :]
    cos, sin = jnp.cos(ang), jnp.sin(ang)
    t = pos.shape[0]
    head = jnp.zeros((t, QK_NOPE), F32)
    tail = jnp.zeros((t, LANES - QK_NOPE - QK_ROPE), F32)
    csk = jnp.concatenate([head, cos, cos, tail], axis=1)
    snk = jnp.concatenate([head, -sin, sin, tail], axis=1)
    log2_scale = ATTN_SCALE * math.log2(math.e)
    csq = jnp.concatenate([head + 1.0, cos, cos, tail], axis=1) * log2_scale
    tables = {"csk": csk, "snk": snk, "csq": csq, "snq": snk * log2_scale}
    return {name: jnp.tile(tab, (reps, 1)) for name, tab in tables.items()}


def _prepare_params(g_mix, g_ffn, g_pe, g_final, w_in_ab, g_qnorm, g_kvnorm, w_uq, w_uk, w_uv,
                    conv_w, conv_b, w_rg, b_rg, w_ig, b_ig, lru_lambda, w_out_ab, w_in_c, ln_g_c,
                    ln_b_c, w_s, b_s, w_out_c, w_gate, w_up, w_down, w_pe, w_pg):
    n_ab = w_in_ab.shape[0]
    o1 = Q_LORA + KV_LORA
    o2 = o1 + QK_ROPE
    pad_tail = LANES - QK_NOPE - QK_ROPE
    w_in = jnp.concatenate(
        [w_in_ab[:, :, :o1], w_in_ab[:, :, o2:], jnp.zeros((n_ab, D_MODEL, QK_NOPE), F32),
         w_in_ab[:, :, o1:o2], jnp.zeros((n_ab, D_MODEL, pad_tail), F32)], axis=2)
    uq = w_uq.reshape(n_ab, Q_LORA, MLA_HEADS, QK_NOPE + QK_ROPE)
    uq = jnp.pad(uq, ((0, 0), (0, 0), (0, 0), (0, pad_tail))).reshape(n_ab, Q_LORA, MLA_HEADS * LANES)
    uk_t = jnp.pad(w_uk.transpose(0, 3, 1, 2), ((0, 0), (0, 0), (0, 0), (0, LANES - QK_NOPE)))
    uk_t = uk_t.reshape(n_ab, KV_LORA, MLA_HEADS * LANES)
    uv_cat = jnp.pad(w_uv.transpose(0, 2, 1, 3), ((0, 0), (0, 0), (0, 0), (0, LANES - V_DIM)))
    uv_cat = uv_cat.reshape(n_ab, KV_LORA, MLA_HEADS * LANES)

    def block_diag(w):
        eye = jnp.eye(LRU_HEADS, dtype=F32)
        return jnp.einsum("jhab,hg->jhagb", w, eye).reshape(n_ab, LRU_WIDTH, LRU_WIDTH)

    vec = lambda a: a[:, None, :]
    return {
        "g_mix": vec(g_mix), "g_ffn": vec(g_ffn), "g_pe": vec(g_pe), "g_final": g_final[None, :],
        "w_in_ab": w_in.astype(BF16), "g_qnorm": vec(g_qnorm), "g_kvnorm": vec(g_kvnorm),
        "w_uq": uq.astype(BF16),
        "w_uk": w_uk.astype(BF16), "w_uv": w_uv.astype(BF16),
        "w_uk_t": uk_t.astype(BF16), "w_uv_cat": uv_cat.astype(BF16),
        "conv_w": conv_w, "conv_b": vec(conv_b),
        "w_gate_lru": jnp.concatenate([block_diag(w_rg), block_diag(w_ig)], axis=2).astype(BF16),
        "b_rg": vec(b_rg), "b_ig": vec(b_ig), "lru_lambda": vec(lru_lambda),
        "w_out_ab": w_out_ab.astype(BF16),
        "w_in_c": w_in_c.astype(BF16), "ln_g_c": vec(ln_g_c), "ln_b_c": vec(ln_b_c),
        "w_s": w_s, "b_s": b_s, "w_out_c": w_out_c.astype(BF16),
        "w_gate": w_gate.astype(BF16), "w_up": w_up.astype(BF16), "w_down": w_down.astype(BF16),
        "w_pe": w_pe.astype(BF16), "w_pg": w_pg.astype(BF16),
    }


def _gmlp_spatial(prm, block_len):
    reps = CHUNK // block_len
    ws = jnp.tile(prm["w_s"][:, :, :block_len, :block_len], (1, 1, reps, reps))
    bs = jnp.tile(prm["b_s"][:, :, :block_len], (1, 1, reps))
    bs = jnp.repeat(jnp.swapaxes(bs, 1, 2), GMLP_GROUP_DIM, axis=2)
    return ws, bs


def _trunk(x, p, tables, n_pos_tiles, prm, past, batch, seq):
    n = batch * seq
    h = x.reshape(n, D_MODEL)
    p = p.reshape(DEPTH, n, PLE_DIM)
    block_len = min(seq, CHUNK)
    ws_tiled, bs_full = _gmlp_spatial(prm, block_len)
    ckv, kpe, lru, conv, vrows = [], [], [], [], []
    for i in range(DEPTH):
        j = i // 2
        if i % 2 == 0:
            if past is None:
                q, k, v, c_kv, k_pe, zx, zg = _in_even(h, tables, prm, j, n_pos_tiles, False)
                attn = _attn_prompt(q, k, v, batch, seq)
                lru_y, h_last, buf = _lru_prompt(zx, zg, prm, j, batch, seq)
                h_last = h_last.reshape(batch, LRU_WIDTH)
            else:
                qcat, kcat, c_kv, k_pe, zx, zg = _in_even(h, tables, prm, j, n_pos_tiles, True)
                q_s = qcat.reshape(MLA_HEADS, batch, seq, QK_CAT).transpose(1, 2, 0, 3)
                q_s = q_s.reshape(batch, seq * MLA_HEADS, QK_CAT)
                knew = jnp.pad(kcat.reshape(batch, seq, QK_CAT),
                               ((0, 0), (0, 2 * SUBLANES - seq), (0, 0)))
                o_lat = _attn_sample(q_s, knew, past["cache_ckv"], past["cache_kpe_t"],
                                     past["page_table"], j)
                attn = _uv_project(o_lat.reshape(n, MLA_HEADS * KV_LORA), prm, j)
                to_t = lambda a: a.reshape(batch, seq, LRU_WIDTH).transpose(1, 0, 2)
                y_t, h_last = _lru_sample(to_t(zx), to_t(zg),
                                          past["state_conv"][j].transpose(1, 0, 2),
                                          past["state_lru"][j], prm, j)
                lru_y = y_t.transpose(1, 0, 2).reshape(n, LRU_WIDTH)
                buf = jnp.concatenate([past["state_conv"][j], zx.reshape(batch, seq, LRU_WIDTH)],
                                      axis=1)[:, seq:]
            h = _post_ab(h, attn, lru_y, p, prm, i)
            ckv.append(c_kv.reshape(batch, seq, KV_LORA))
            kpe.append(k_pe.reshape(batch, seq, QK_ROPE))
            lru.append(h_last)
            conv.append(buf)
        else:
            h, v = _post_c(h, p, prm, i, ws_tiled, bs_full, block_len, past is not None)
            if v is not None:
                vrows.append(v.reshape(batch, seq, GMLP_WIDTH))
    return h.reshape(batch, seq, D_MODEL), ckv, kpe, lru, conv, vrows


def kernel(x_prompt, x_sample, cache_ckv, cache_kpe, state_lru, state_conv, page_table, p_prompt, p_sample, g_mix, g_ffn, g_pe, g_final, w_in_ab, g_qnorm, g_kvnorm, w_uq, w_uk, w_uv, conv_w, conv_b, w_rg, b_rg, w_ig, b_ig, lru_lambda, w_out_ab, w_in_c, ln_g_c, ln_b_c, w_s, b_s, w_out_c, w_gate, w_up, w_down, w_pe, w_pg):
    prm = _prepare_params(g_mix, g_ffn, g_pe, g_final, w_in_ab, g_qnorm, g_kvnorm, w_uq, w_uk,
                          w_uv, conv_w, conv_b, w_rg, b_rg, w_ig, b_ig, lru_lambda, w_out_ab,
                          w_in_c, ln_g_c, ln_b_c, w_s, b_s, w_out_c, w_gate, w_up, w_down, w_pe,
                          w_pg)
    batch, seq, _ = x_prompt.shape
    dec_batch, dec_seq, _ = x_sample.shape
    past_len = page_table.shape[1] * PAGE_SIZE

    tables_p = _rope_tables(jnp.arange(seq, dtype=jnp.int32))
    y_prompt, ckv_p, kpe_p, lru_p, conv_p, _ = _trunk(
        x_prompt, p_prompt, tables_p, seq // ROW_TILE, prm, None, batch, seq)

    tables_s = _rope_tables(past_len + jnp.arange(dec_seq, dtype=jnp.int32), ROW_TILE // dec_seq)
    past = {"cache_ckv": cache_ckv, "cache_kpe_t": jnp.swapaxes(cache_kpe, 2, 3), "state_lru": state_lru,
            "state_conv": state_conv, "page_table": page_table}
    y_sample, ckv_s, kpe_s, lru_s, conv_s, v_s = _trunk(
        x_sample, p_sample, tables_s, 1, prm, past, dec_batch, dec_seq)

    return (y_prompt, y_sample,
            jnp.stack(ckv_p), jnp.stack(kpe_p), jnp.stack(lru_p), jnp.stack(conv_p),
            jnp.stack(ckv_s), jnp.stack(kpe_s), jnp.stack(lru_s), jnp.stack(conv_s),
            jnp.stack(v_s))
```

```python
import functools
import math

import jax
import jax.numpy as jnp
from jax import lax
from jax.experimental import pallas as pl
from jax.experimental.pallas import tpu as pltpu

F32 = jnp.float32
BF16 = jnp.bfloat16

D_MODEL = 1024
DEPTH = 4
PAGE_SIZE = 128
MLA_HEADS = 8
Q_LORA = 384
KV_LORA = 256
QK_NOPE = 64
QK_ROPE = 32
ROPE_HALF = QK_ROPE // 2
V_DIM = 64
ROPE_THETA = 10000.0
ATTN_SCALE = 1.0 / math.sqrt(QK_NOPE + QK_ROPE)
LRU_WIDTH = 512
LRU_HEADS = 8
LRU_HEAD_DIM = LRU_WIDTH // LRU_HEADS
CONV_WIDTH = 4
LRU_C = 8.0
CHUNK = 128
GMLP_WIDTH = 1024
GMLP_GROUPS = 8
GMLP_GROUP_DIM = GMLP_WIDTH // GMLP_GROUPS
FFN_HIDDEN = -(-8 * D_MODEL // (3 * 256)) * 256
PLE_DIM = 256
EPS = 1e-6

LANES = 128
SUBLANES = 8
QK_CAT = KV_LORA + LANES
IN_AB_PAD = Q_LORA + KV_LORA + 2 * LRU_WIDTH + LANES
VMEM_LIMIT = 60 * 1024 * 1024

IN_TILE = 512
POST_AB_TILE = 512
POST_C_TILE = 512
ATTN_TILE = 256
LRU_TILE = 256
SAMPLE_KEY_SPANS = 4


def _const_spec(shape):
    nd = len(shape)
    return pl.BlockSpec(shape, lambda *_: (0,) * nd, pipeline_mode=pl.Buffered(1))


def _layer_spec(shape, layer):
    nd = len(shape)
    return pl.BlockSpec((None,) + tuple(shape), lambda *_: (layer,) + (0,) * nd,
                        pipeline_mode=pl.Buffered(1))


def _rms(x, g):
    return x * lax.rsqrt(jnp.mean(x * x, axis=-1, keepdims=True) + EPS) * g


def _dot(a, b):
    return jnp.dot(a, b, preferred_element_type=F32)


def _dot_nt(a, b):
    return lax.dot_general(a, b, (((1,), (1,)), ((), ())), preferred_element_type=F32)


def _rope_block(blk, cs, sn):
    partner = pltpu.roll(blk, LANES - ROPE_HALF, 1)
    return blk * cs + partner * sn


def _in_even_common(h_ref, csk_ref, snk_ref, gmix_ref, win_ref, gq_ref, wuq_ref, gkv_ref,
                    ckv_ref, kpe_ref, zx_ref, zg_ref):
    xn = _rms(h_ref[...], gmix_ref[...]).astype(BF16)
    z = _dot(xn, win_ref[...])
    o_kv = Q_LORA
    o_x = o_kv + KV_LORA
    o_g = o_x + LRU_WIDTH
    o_pe = o_g + LRU_WIDTH
    zx_ref[...] = z[:, o_x:o_g]
    zg_ref[...] = z[:, o_g:o_pe]
    ckv = _rms(z[:, o_kv:o_x], gkv_ref[...])
    kpe_blk = _rope_block(z[:, o_pe:o_pe + LANES], csk_ref[...], snk_ref[...])
    ckv_ref[...] = ckv
    kpe_ref[...] = kpe_blk[:, QK_NOPE:QK_NOPE + QK_ROPE]
    qn = _rms(z[:, :Q_LORA], gq_ref[...]).astype(BF16)
    q = _dot(qn, wuq_ref[...])
    return ckv, kpe_blk, q


def _in_even_sample_kernel(h_ref, csk_ref, snk_ref, gmix_ref, win_ref, gq_ref, wuq_ref, gkv_ref,
                           wuk_ref, qcat_ref, kcat_ref, ckv_ref, kpe_ref, zx_ref, zg_ref):
    ckv, kpe_blk, q = _in_even_common(h_ref, csk_ref, snk_ref, gmix_ref, win_ref, gq_ref, wuq_ref,
                                      gkv_ref, ckv_ref, kpe_ref, zx_ref, zg_ref)
    kcat_ref[:, :KV_LORA] = ckv.astype(BF16)
    kcat_ref[:, KV_LORA:] = kpe_blk.astype(BF16)
    for hh in range(MLA_HEADS):
        blk = q[:, hh * LANES:(hh + 1) * LANES]
        qcat_ref[hh, :, :KV_LORA] = _dot(blk[:, :QK_NOPE].astype(BF16), wuk_ref[hh]).astype(BF16)
        qcat_ref[hh, :, KV_LORA:] = _rope_block(blk, csk_ref[...], snk_ref[...]).astype(BF16)


def _in_even_prompt_kernel(h_ref, csk_ref, snk_ref, gmix_ref, win_ref, gq_ref, wuq_ref, gkv_ref,
                           csq_ref, snq_ref, wukt_ref, wuvc_ref,
                           q_ref, k_ref, v_ref, ckv_ref, kpe_ref, zx_ref, zg_ref):
    ckv, kpe_blk, q = _in_even_common(h_ref, csk_ref, snk_ref, gmix_ref, win_ref, gq_ref, wuq_ref,
                                      gkv_ref, ckv_ref, kpe_ref, zx_ref, zg_ref)
    cb = ckv.astype(BF16)
    k_nope = _dot(cb, wukt_ref[...])
    v = _dot(cb, wuvc_ref[...])
    ones_lane = (lax.broadcasted_iota(jnp.int32, (1, LANES), 1) == V_DIM).astype(F32)
    for hh in range(MLA_HEADS):
        blk = slice(hh * LANES, (hh + 1) * LANES)
        q_ref[hh] = _rope_block(q[:, blk], csq_ref[...], snq_ref[...]).astype(BF16)
        k_ref[hh] = (k_nope[:, blk] + kpe_blk).astype(BF16)
        v_ref[hh] = (v[:, blk] + ones_lane).astype(BF16)


def _in_even(h, tables, prm, j, n_pos_tiles, absorbed):
    n = h.shape[0]
    tm = IN_TILE
    row = lambda w: pl.BlockSpec((tm, w), lambda r: (r, 0))
    heads = lambda w: pl.BlockSpec((MLA_HEADS, tm, w), lambda r: (0, r, 0))
    pos = pl.BlockSpec((tm, LANES), lambda r: (r % n_pos_tiles, 0))
    in_specs = [
        row(D_MODEL), pos, pos,
        _layer_spec((1, D_MODEL), 2 * j),
        _layer_spec((D_MODEL, IN_AB_PAD), j),
        _layer_spec((1, Q_LORA), j),
        _layer_spec((Q_LORA, MLA_HEADS * LANES), j),
        _layer_spec((1, KV_LORA), j),
    ]
    args = [h, tables["csk"], tables["snk"], prm["g_mix"], prm["w_in_ab"], prm["g_qnorm"],
            prm["w_uq"], prm["g_kvnorm"]]
    tail_specs = (row(KV_LORA), row(QK_ROPE), row(LRU_WIDTH), row(LRU_WIDTH))
    tail_shapes = (
        jax.ShapeDtypeStruct((n, KV_LORA), F32),
        jax.ShapeDtypeStruct((n, QK_ROPE), F32),
        jax.ShapeDtypeStruct((n, LRU_WIDTH), F32),
        jax.ShapeDtypeStruct((n, LRU_WIDTH), F32),
    )
    if absorbed:
        body = _in_even_sample_kernel
        in_specs += [_layer_spec((MLA_HEADS, QK_NOPE, KV_LORA), j)]
        args += [prm["w_uk"]]
        out_specs = (heads(QK_CAT), row(QK_CAT)) + tail_specs
        out_shape = (jax.ShapeDtypeStruct((MLA_HEADS, n, QK_CAT), BF16),
                     jax.ShapeDtypeStruct((n, QK_CAT), BF16)) + tail_shapes
    else:
        body = _in_even_prompt_kernel
        in_specs += [pos, pos,
                     _layer_spec((KV_LORA, MLA_HEADS * LANES), j),
                     _layer_spec((KV_LORA, MLA_HEADS * LANES), j)]
        args += [tables["csq"], tables["snq"], prm["w_uk_t"], prm["w_uv_cat"]]
        out_specs = (heads(LANES), heads(LANES), heads(LANES)) + tail_specs
        out_shape = (jax.ShapeDtypeStruct((MLA_HEADS, n, LANES), BF16),) * 3 + tail_shapes
    return pl.pallas_call(
        body,
        grid=(n // tm,),
        in_specs=in_specs,
        out_specs=out_specs,
        out_shape=out_shape,
        compiler_params=pltpu.CompilerParams(dimension_semantics=("parallel",),
                                             vmem_limit_bytes=VMEM_LIMIT),
        name="in_even_absorbed" if absorbed else "in_even_heads",
    )(*args)


def _lru_gates(xc, wgate_ref, brg_ref, big_ref, lam_ref):
    g = _dot(xc.astype(BF16), wgate_ref[...])
    r = jax.nn.sigmoid(g[:, :LRU_WIDTH] + brg_ref[...])
    ig = jax.nn.sigmoid(g[:, LRU_WIDTH:] + big_ref[...])
    x = -lam_ref[...]
    softplus = jnp.maximum(x, 0.0) + jnp.log1p(jnp.exp(-jnp.abs(x)))
    log_a = -LRU_C * r * softplus
    a = jnp.exp(log_a)
    th = jnp.tanh(log_a)
    bx = jnp.sqrt(-2.0 * th / (1.0 - th)) * ig * xc
    return a, bx


def _shift_rows(x, s, fill):
    return jnp.concatenate([jnp.full((s, x.shape[1]), fill, x.dtype), x[:-s]], axis=0)


def _lru_prompt_kernel(zx_ref, zg_ref, cw_ref, cb_ref, wgate_ref, brg_ref, big_ref, lam_ref,
                       y_ref, hlast_ref, conv_ref, xbuf, hprev):
    t = pl.program_id(1)
    tl = zx_ref.shape[0]
    pad = SUBLANES

    @pl.when(t == 0)
    def _():
        xbuf[:pad, :] = jnp.zeros((pad, LRU_WIDTH), F32)
        hprev[...] = jnp.zeros_like(hprev)

    x = zx_ref[...]
    xbuf[pad:, :] = x
    xc = cb_ref[...] + xbuf[pad - 3:pad - 3 + tl, :] * cw_ref[0:1, :]
    xc = xc + xbuf[pad - 2:pad - 2 + tl, :] * cw_ref[1:2, :]
    xc = xc + xbuf[pad - 1:pad - 1 + tl, :] * cw_ref[2:3, :]
    xc = xc + x * cw_ref[3:4, :]
    xbuf[:pad, :] = x[tl - pad:, :]

    a, b = _lru_gates(xc, wgate_ref, brg_ref, big_ref, lam_ref)
    s = 1
    while s < tl:
        b = a * _shift_rows(b, s, 0.0) + b
        a = a * _shift_rows(a, s, 1.0)
        s *= 2
    hs = a * hprev[...] + b
    hprev[...] = hs[tl - 1:, :]
    y_ref[...] = (hs * jax.nn.gelu(zg_ref[...])).astype(y_ref.dtype)

    @pl.when(t == pl.num_programs(1) - 1)
    def _():
        hlast_ref[...] = hs[tl - 1:, :]
        conv_ref[...] = x[tl - (CONV_WIDTH - 1):, :]


def _lru_prompt(zx, zg, prm, j, batch, seq):
    tl = LRU_TILE
    nt = seq // tl
    row = pl.BlockSpec((tl, LRU_WIDTH), lambda b, t: (b * nt + t, 0))
    return pl.pallas_call(
        _lru_prompt_kernel,
        grid=(batch, nt),
        in_specs=[
            row, row,
            _layer_spec((CONV_WIDTH, LRU_WIDTH), j),
            _layer_spec((1, LRU_WIDTH), j),
            _layer_spec((LRU_WIDTH, 2 * LRU_WIDTH), j),
            _layer_spec((1, LRU_WIDTH), j),
            _layer_spec((1, LRU_WIDTH), j),
            _layer_spec((1, LRU_WIDTH), j),
        ],
        out_specs=(
            row,
            pl.BlockSpec((None, 1, LRU_WIDTH), lambda b, t: (b, 0, 0)),
            pl.BlockSpec((None, CONV_WIDTH - 1, LRU_WIDTH), lambda b, t: (b, 0, 0)),
        ),
        out_shape=(
            jax.ShapeDtypeStruct((batch * seq, LRU_WIDTH), BF16),
            jax.ShapeDtypeStruct((batch, 1, LRU_WIDTH), F32),
            jax.ShapeDtypeStruct((batch, CONV_WIDTH - 1, LRU_WIDTH), F32),
        ),
        scratch_shapes=[pltpu.VMEM((SUBLANES + tl, LRU_WIDTH), F32),
                        pltpu.VMEM((1, LRU_WIDTH), F32)],
        compiler_params=pltpu.CompilerParams(dimension_semantics=("parallel", "arbitrary"),
                                             vmem_limit_bytes=VMEM_LIMIT),
        name="lru_prompt",
    )(zx, zg, prm["conv_w"], prm["conv_b"], prm["w_gate_lru"], prm["b_rg"], prm["b_ig"],
      prm["lru_lambda"])


def _lru_sample_kernel(zx_ref, zg_ref, buf_ref, h0_ref, cw_ref, cb_ref, wgate_ref, brg_ref,
                       big_ref, lam_ref, y_ref, hlast_ref):
    nt = zx_ref.shape[0]
    nb = zx_ref.shape[1]
    xpad = [buf_ref[k] for k in range(CONV_WIDTH - 1)] + [zx_ref[k] for k in range(nt)]
    xcs = []
    for t in range(nt):
        xc = cb_ref[...] + xpad[t] * cw_ref[0:1, :]
        for k in range(1, CONV_WIDTH):
            xc = xc + xpad[t + k] * cw_ref[k:k + 1, :]
        xcs.append(xc)
    a, b = _lru_gates(jnp.concatenate(xcs, axis=0), wgate_ref, brg_ref, big_ref, lam_ref)
    h = h0_ref[...]
    for t in range(nt):
        h = a[t * nb:(t + 1) * nb] * h + b[t * nb:(t + 1) * nb]
        y_ref[t] = (h * jax.nn.gelu(zg_ref[t])).astype(y_ref.dtype)
    hlast_ref[...] = h


def _lru_sample(zx_t, zg_t, buf_t, h0, prm, j):
    nt, nb, _ = zx_t.shape
    return pl.pallas_call(
        _lru_sample_kernel,
        grid=(1,),
        in_specs=[
            _const_spec((nt, nb, LRU_WIDTH)), _const_spec((nt, nb, LRU_WIDTH)),
            _const_spec((CONV_WIDTH - 1, nb, LRU_WIDTH)), _const_spec((nb, LRU_WIDTH)),
            _layer_spec((CONV_WIDTH, LRU_WIDTH), j),
            _layer_spec((1, LRU_WIDTH), j),
            _layer_spec((LRU_WIDTH, 2 * LRU_WIDTH), j),
            _layer_spec((1, LRU_WIDTH), j),
            _layer_spec((1, LRU_WIDTH), j),
            _layer_spec((1, LRU_WIDTH), j),
        ],
        out_specs=(
            pl.BlockSpec((nt, nb, LRU_WIDTH), lambda i: (0, 0, 0)),
            pl.BlockSpec((nb, LRU_WIDTH), lambda i: (0, 0)),
        ),
        out_shape=(
            jax.ShapeDtypeStruct((nt, nb, LRU_WIDTH), BF16),
            jax.ShapeDtypeStruct((nb, LRU_WIDTH), F32),
        ),
        compiler_params=pltpu.CompilerParams(dimension_semantics=("arbitrary",),
                                             vmem_limit_bytes=VMEM_LIMIT),
        name="lru_sample",
    )(zx_t, zg_t, buf_t, h0, prm["conv_w"], prm["conv_b"], prm["w_gate_lru"], prm["b_rg"],
      prm["b_ig"], prm["lru_lambda"])


def _attn_prompt_kernel(q_ref, k_ref, v_ref, o_ref, m_sc, acc_sc):
    i = pl.program_id(1)
    tq = q_ref.shape[1]
    m_sc[...] = jnp.full(m_sc.shape, -jnp.inf, F32)
    acc_sc[...] = jnp.zeros(acc_sc.shape, F32)

    def key_block(start, width, masked):
        keys = pl.ds(start, width)
        if masked:
            q_pos = i * tq + lax.broadcasted_iota(jnp.int32, (tq, width), 0)
            causal = start + lax.broadcasted_iota(jnp.int32, (tq, width), 1) <= q_pos
        scores = [_dot_nt(q_ref[hh], k_ref[hh, keys, :]) for hh in range(MLA_HEADS)]
        for hh in range(MLA_HEADS):
            s = scores[hh]
            if masked:
                s = jnp.where(causal, s, -jnp.inf)
            m_prev = m_sc[hh]
            m_new = jnp.maximum(m_prev, jnp.max(s, axis=-1, keepdims=True))
            alpha = jnp.exp2(m_prev - m_new)
            p = jnp.exp2(s - jnp.concatenate([m_new] * (width // LANES), axis=1))
            acc_sc[hh] = alpha * acc_sc[hh] + _dot(p.astype(BF16), v_ref[hh, keys, :])
            m_sc[hh] = m_new

    def full_pair(jb, carry):
        key_block(pl.multiple_of(jb * (2 * tq), 2 * tq), 2 * tq, False)
        return carry

    lax.fori_loop(0, i // 2, full_pair, 0)

    @pl.when(i % 2 == 1)
    def _():
        key_block(pl.multiple_of((i - 1) * tq, 2 * tq), 2 * tq, True)

    @pl.when(i % 2 == 0)
    def _():
        key_block(pl.multiple_of(i * tq, tq), tq, True)

    outs = []
    for hh in range(MLA_HEADS):
        acc = acc_sc[hh]
        outs.append(acc[:, :V_DIM] / acc[:, V_DIM:V_DIM + 1])
    o_ref[...] = jnp.concatenate(outs, axis=-1).astype(o_ref.dtype)


def _attn_prompt(q, k, v, batch, seq):
    tq = ATTN_TILE
    nq = seq // tq
    return pl.pallas_call(
        _attn_prompt_kernel,
        grid=(batch, nq),
        in_specs=[
            pl.BlockSpec((MLA_HEADS, tq, LANES), lambda b, i: (0, b * nq + i, 0)),
            pl.BlockSpec((MLA_HEADS, seq, LANES), lambda b, i: (0, b, 0)),
            pl.BlockSpec((MLA_HEADS, seq, LANES), lambda b, i: (0, b, 0)),
        ],
        out_specs=pl.BlockSpec((tq, MLA_HEADS * V_DIM), lambda b, i: (b * nq + i, 0)),
        out_shape=jax.ShapeDtypeStruct((batch * seq, MLA_HEADS * V_DIM), BF16),
        scratch_shapes=[pltpu.VMEM((MLA_HEADS, tq, LANES), F32),
                        pltpu.VMEM((MLA_HEADS, tq, LANES), F32)],
        compiler_params=pltpu.CompilerParams(dimension_semantics=("parallel", "arbitrary"),
                                             vmem_limit_bytes=VMEM_LIMIT),
        name="attn_prompt",
    )(q, k, v)


def _attn_sample_kernel(pt_ref, q_ref, knew_ref, ckv_hbm, kpe_hbm, o_ref, cbuf, pbuf, sems, *,
                        layer, n_pages):
    b = pl.program_id(0)
    nb = pl.num_programs(0)
    slot = b % 2

    def page_copies(bb, sl):
        copies = []
        for pg in range(n_pages):
            page = pt_ref[bb * n_pages + pg]
            rows = pl.ds(pg * PAGE_SIZE, PAGE_SIZE)
            copies.append(pltpu.make_async_copy(ckv_hbm.at[layer, page], cbuf.at[sl, rows],
                                                sems.at[0, sl]))
            copies.append(pltpu.make_async_copy(kpe_hbm.at[layer, page], pbuf.at[sl, :, rows],
                                                sems.at[1, sl]))
        return copies

    @pl.when(b == 0)
    def _():
        for cp in page_copies(b, slot):
            cp.start()

    @pl.when(b + 1 < nb)
    def _():
        for cp in page_copies(b + 1, 1 - slot):
            cp.start()

    for cp in page_copies(b, slot):
        cp.wait()

    q = q_ref[...]
    q_lat = q[:, :KV_LORA]
    o_pe = KV_LORA + QK_NOPE
    q_pe = q[:, o_pe:o_pe + QK_ROPE]
    knew = knew_ref[...]
    s_new = _dot_nt(q, knew) * ATTN_SCALE
    t_row = lax.broadcasted_iota(jnp.int32, s_new.shape, 0) // MLA_HEADS
    t_key = lax.broadcasted_iota(jnp.int32, s_new.shape, 1)
    s_new = jnp.where(t_key <= t_row, s_new, -jnp.inf)
    span = cbuf.shape[1] // SAMPLE_KEY_SPANS
    spans = [slice(c * span, (c + 1) * span) for c in range(SAMPLE_KEY_SPANS)]
    kcs = [cbuf[slot, sp, :].astype(BF16) for sp in spans]
    s_lat = [_dot_nt(q_lat, kc) for kc in kcs]
    s_pe = [_dot(q_pe, pbuf[slot, :, sp].astype(BF16)) for sp in spans]
    scores = [(a + b) * ATTN_SCALE for a, b in zip(s_lat, s_pe)]
    m = jnp.max(s_new, axis=-1, keepdims=True)
    for s in scores:
        m = jnp.maximum(m, jnp.max(s, axis=-1, keepdims=True))
    p_new = jnp.exp(s_new - m)
    denom = jnp.sum(p_new, axis=-1, keepdims=True)
    outs = [_dot(p_new.astype(BF16), knew[:, :KV_LORA])]
    for s, kc in zip(scores, kcs):
        p = jnp.exp(s - m)
        denom = denom + jnp.sum(p, axis=-1, keepdims=True)
        outs.append(_dot(p.astype(BF16), kc))
    while len(outs) > 1:
        outs = [outs[k] + outs[k + 1] if k + 1 < len(outs) else outs[k]
                for k in range(0, len(outs), 2)]
    o_ref[...] = (outs[0] / denom).astype(o_ref.dtype)


def _attn_sample(q_s, knew, cache_ckv, cache_kpe_t, page_table, j):
    nb, rows, _ = q_s.shape
    n_pages = page_table.shape[1]
    past = n_pages * PAGE_SIZE
    kernel = functools.partial(_attn_sample_kernel, layer=j, n_pages=n_pages)
    grid_spec = pltpu.PrefetchScalarGridSpec(
        num_scalar_prefetch=1,
        grid=(nb,),
        in_specs=[
            pl.BlockSpec((None, rows, QK_CAT), lambda b, pt: (b, 0, 0)),
            pl.BlockSpec((None, knew.shape[1], QK_CAT), lambda b, pt: (b, 0, 0)),
            pl.BlockSpec(memory_space=pl.ANY),
            pl.BlockSpec(memory_space=pl.ANY),
        ],
        out_specs=pl.BlockSpec((None, rows, KV_LORA), lambda b, pt: (b, 0, 0)),
        scratch_shapes=[
            pltpu.VMEM((2, past, KV_LORA), F32),
            pltpu.VMEM((2, QK_ROPE, past), F32),
            pltpu.SemaphoreType.DMA((2, 2)),
        ],
    )
    return pl.pallas_call(
        kernel,
        grid_spec=grid_spec,
        out_shape=jax.ShapeDtypeStruct((nb, rows, KV_LORA), BF16),
        compiler_params=pltpu.CompilerParams(dimension_semantics=("arbitrary",),
                                             vmem_limit_bytes=VMEM_LIMIT),
        name="attn_sample",
    )(page_table.reshape(-1), q_s, knew, cache_ckv, cache_kpe_t)


def _uv_kernel(o_ref, wuv_ref, a_ref):
    outs = [_dot(o_ref[:, hh * KV_LORA:(hh + 1) * KV_LORA], wuv_ref[hh])
            for hh in range(MLA_HEADS)]
    a_ref[...] = jnp.concatenate(outs, axis=-1).astype(a_ref.dtype)


def _uv_project(o_lat, prm, j):
    n = o_lat.shape[0]
    return pl.pallas_call(
        _uv_kernel,
        grid=(1,),
        in_specs=[_const_spec((n, MLA_HEADS * KV_LORA)),
                  _layer_spec((MLA_HEADS, KV_LORA, V_DIM), j)],
        out_specs=pl.BlockSpec((n, MLA_HEADS * V_DIM), lambda i: (0, 0)),
        out_shape=jax.ShapeDtypeStruct((n, MLA_HEADS * V_DIM), BF16),
        compiler_params=pltpu.CompilerParams(dimension_semantics=("arbitrary",),
                                             vmem_limit_bytes=VMEM_LIMIT),
        name="uv_project",
    )(o_lat, prm["w_uv"])


def _ffn_and_gate(h1, p_ref, gffn_ref, wg_ref, wu_ref, wd_ref, gpe_ref, wpg_ref, wpe_ref):
    hn = _rms(h1, gffn_ref[...]).astype(BF16)
    act = (jax.nn.silu(_dot(hn, wg_ref[...])) * _dot(hn, wu_ref[...])).astype(BF16)
    h2 = h1 + _dot(act, wd_ref[...])
    gate = jax.nn.sigmoid(_dot(_rms(h2, gpe_ref[...]).astype(BF16), wpg_ref[...]))
    return h2 + _dot(p_ref[...].astype(BF16), wpe_ref[...]) * gate


def _post_ab_kernel(h_ref, attn_ref, lru_ref, p_ref, wout_ref, gffn_ref, wg_ref, wu_ref, wd_ref,
                    gpe_ref, wpg_ref, wpe_ref, out_ref):
    n_attn = MLA_HEADS * V_DIM
    mix = _dot(attn_ref[...], wout_ref[:n_attn, :]) + _dot(lru_ref[...], wout_ref[n_attn:, :])
    h1 = h_ref[...] + mix
    out_ref[...] = _ffn_and_gate(h1, p_ref, gffn_ref, wg_ref, wu_ref, wd_ref, gpe_ref, wpg_ref,
                                 wpe_ref)


def _post_c_kernel(h_ref, p_ref, gmix_ref, winc_ref, lng_ref, lnb_ref, ws_ref, bs_ref, woutc_ref,
                   gffn_ref, wg_ref, wu_ref, wd_ref, gpe_ref, wpg_ref, wpe_ref, gfin_ref,
                   out_ref, *rest, block_len, final):
    sp_sc = rest[-1]
    h = h_ref[...]
    tm = h.shape[0]
    z = jax.nn.gelu(_dot(_rms(h, gmix_ref[...]).astype(BF16), winc_ref[...]))
    u = z[:, :GMLP_WIDTH]
    v = z[:, GMLP_WIDTH:]
    vc = v - jnp.mean(v, axis=-1, keepdims=True)
    v = vc * lax.rsqrt(jnp.mean(vc * vc, axis=-1, keepdims=True) + EPS) * lng_ref[...] + lnb_ref[...]
    if len(rest) == 2:
        rest[0][...] = v
    vb = v.astype(BF16)
    t_idx = lax.broadcasted_iota(jnp.int32, (CHUNK, CHUNK), 0)
    s_idx = lax.broadcasted_iota(jnp.int32, (CHUNK, CHUNK), 1)
    mask = (s_idx <= t_idx) & ((s_idx // block_len) == (t_idx // block_len))
    for gg in range(GMLP_GROUPS):
        wm = jnp.where(mask, ws_ref[gg], 0.0).astype(BF16)
        cols = slice(gg * GMLP_GROUP_DIM, (gg + 1) * GMLP_GROUP_DIM)
        for cc in range(tm // CHUNK):
            rows = slice(cc * CHUNK, (cc + 1) * CHUNK)
            sp_sc[rows, cols] = _dot(wm, vb[rows, cols]) + bs_ref[:, cols]
    mix = _dot((u * sp_sc[...]).astype(BF16), woutc_ref[...])
    h3 = _ffn_and_gate(h + mix, p_ref, gffn_ref, wg_ref, wu_ref, wd_ref, gpe_ref, wpg_ref, wpe_ref)
    out_ref[...] = _rms(h3, gfin_ref[...]) if final else h3


def _ffn_specs(i):
    return [
        _layer_spec((1, D_MODEL), i),
        _layer_spec((D_MODEL, FFN_HIDDEN), i),
        _layer_spec((D_MODEL, FFN_HIDDEN), i),
        _layer_spec((FFN_HIDDEN, D_MODEL), i),
        _layer_spec((1, D_MODEL), i),
        _layer_spec((D_MODEL, D_MODEL), i),
        _layer_spec((PLE_DIM, D_MODEL), i),
    ]


def _ffn_args(prm):
    return (prm["g_ffn"], prm["w_gate"], prm["w_up"], prm["w_down"], prm["g_pe"], prm["w_pg"],
            prm["w_pe"])


def _post_ab(h, attn, lru_y, p, prm, i):
    n = h.shape[0]
    tm = POST_AB_TILE
    j = i // 2
    row = lambda w: pl.BlockSpec((tm, w), lambda r: (r, 0))
    return pl.pallas_call(
        _post_ab_kernel,
        grid=(n // tm,),
        in_specs=[
            row(D_MODEL), row(MLA_HEADS * V_DIM), row(LRU_WIDTH),
            pl.BlockSpec((None, tm, PLE_DIM), lambda r: (i, r, 0)),
            _layer_spec((MLA_HEADS * V_DIM + LRU_WIDTH, D_MODEL), j),
        ] + _ffn_specs(i),
        out_specs=row(D_MODEL),
        out_shape=jax.ShapeDtypeStruct((n, D_MODEL), F32),
        compiler_params=pltpu.CompilerParams(dimension_semantics=("parallel",),
                                             vmem_limit_bytes=VMEM_LIMIT),
        name="post_ab",
    )(h, attn, lru_y, p, prm["w_out_ab"], *_ffn_args(prm))


def _post_c(h, p, prm, i, ws_tiled, bs_full, block_len, emit_v):
    n = h.shape[0]
    tm = POST_C_TILE
    j = i // 2
    final = i == DEPTH - 1
    row = lambda w: pl.BlockSpec((tm, w), lambda r: (r, 0))
    kernel = functools.partial(_post_c_kernel, block_len=block_len, final=final)
    out_specs = [row(D_MODEL)] + ([row(GMLP_WIDTH)] if emit_v else [])
    out_shape = [jax.ShapeDtypeStruct((n, D_MODEL), F32)]
    if emit_v:
        out_shape.append(jax.ShapeDtypeStruct((n, GMLP_WIDTH), F32))
    outs = pl.pallas_call(
        kernel,
        grid=(n // tm,),
        in_specs=[
            row(D_MODEL),
            pl.BlockSpec((None, tm, PLE_DIM), lambda r: (i, r, 0)),
            _layer_spec((1, D_MODEL), i),
            _layer_spec((D_MODEL, 2 * GMLP_WIDTH), j),
            _layer_spec((1, GMLP_WIDTH), j),
            _layer_spec((1, GMLP_WIDTH), j),
            _layer_spec((GMLP_GROUPS, CHUNK, CHUNK), j),
            _layer_spec((CHUNK, GMLP_WIDTH), j),
            _layer_spec((GMLP_WIDTH, D_MODEL), j),
        ] + _ffn_specs(i) + [_const_spec((1, D_MODEL))],
        out_specs=out_specs,
        out_shape=out_shape,
        scratch_shapes=[pltpu.VMEM((tm, GMLP_WIDTH), F32)],
        compiler_params=pltpu.CompilerParams(dimension_semantics=("parallel",),
                                             vmem_limit_bytes=VMEM_LIMIT),
        name="post_c",
    )(h, p, prm["g_mix"], prm["w_in_c"], prm["ln_g_c"], prm["ln_b_c"], ws_tiled, bs_full,
      prm["w_out_c"], *_ffn_args(prm), prm["g_final"])
    return outs[0], (outs[1] if emit_v else None)


def _rope_tables(pos, reps=1):
    inv = jnp.exp(-math.log(ROPE_THETA) * jnp.arange(ROPE_HALF, dtype=F32) / ROPE_HALF)
    ang = pos.astype(F32)[:, None] * inv[None, :]
    cos, sin = jnp.cos(ang), jnp.sin(ang)
    t = pos.shape[0]
    head = jnp.zeros((t, QK_NOPE), F32)
    tail = jnp.zeros((t, LANES - QK_NOPE - QK_ROPE), F32)
    csk = jnp.concatenate([head, cos, cos, tail], axis=1)
    snk = jnp.concatenate([head, -sin, sin, tail], axis=1)
    log2_scale = ATTN_SCALE * math.log2(math.e)
    csq = jnp.concatenate([head + 1.0, cos, cos, tail], axis=1) * log2_scale
    tables = {"csk": csk, "snk": snk, "csq": csq, "snq": snk * log2_scale}
    return {name: jnp.tile(tab, (reps, 1)) for name, tab in tables.items()}


def _prepare_params(g_mix, g_ffn, g_pe, g_final, w_in_ab, g_qnorm, g_kvnorm, w_uq, w_uk, w_uv,
                    conv_w, conv_b, w_rg, b_rg, w_ig, b_ig, lru_lambda, w_out_ab, w_in_c, ln_g_c,
                    ln_b_c, w_s, b_s, w_out_c, w_gate, w_up, w_down, w_pe, w_pg):
    n_ab = w_in_ab.shape[0]
    o1 = Q_LORA + KV_LORA
    o2 = o1 + QK_ROPE
    pad_tail = LANES - QK_NOPE - QK_ROPE - ROPE_HALF
    w_in = jnp.concatenate(
        [w_in_ab[:, :, :o1], w_in_ab[:, :, o2:], jnp.zeros((n_ab, D_MODEL, QK_NOPE), F32),
         w_in_ab[:, :, o1:o2], w_in_ab[:, :, o1:o1 + ROPE_HALF],
         jnp.zeros((n_ab, D_MODEL, pad_tail), F32)], axis=2)
    uq = w_uq.reshape(n_ab, Q_LORA, MLA_HEADS, QK_NOPE + QK_ROPE)
    uq = jnp.concatenate([uq, uq[..., QK_NOPE:QK_NOPE + ROPE_HALF],
                          jnp.zeros(uq.shape[:3] + (pad_tail,), F32)], axis=3)
    uq = uq.reshape(n_ab, Q_LORA, MLA_HEADS * LANES)
    uk_t = jnp.pad(w_uk.transpose(0, 3, 1, 2), ((0, 0), (0, 0), (0, 0), (0, LANES - QK_NOPE)))
    uk_t = uk_t.reshape(n_ab, KV_LORA, MLA_HEADS * LANES)
    uv_cat = jnp.pad(w_uv.transpose(0, 2, 1, 3), ((0, 0), (0, 0), (0, 0), (0, LANES - V_DIM)))
    uv_cat = uv_cat.reshape(n_ab, KV_LORA, MLA_HEADS * LANES)

    def block_diag(w):
        eye = jnp.eye(LRU_HEADS, dtype=F32)
        return jnp.einsum("jhab,hg->jhagb", w, eye).reshape(n_ab, LRU_WIDTH, LRU_WIDTH)

    vec = lambda a: a[:, None, :]
    return {
        "g_mix": vec(g_mix), "g_ffn": vec(g_ffn), "g_pe": vec(g_pe), "g_final": g_final[None, :],
        "w_in_ab": w_in.astype(BF16), "g_qnorm": vec(g_qnorm), "g_kvnorm": vec(g_kvnorm),
        "w_uq": uq.astype(BF16),
        "w_uk": w_uk.astype(BF16), "w_uv": w_uv.astype(BF16),
        "w_uk_t": uk_t.astype(BF16), "w_uv_cat": uv_cat.astype(BF16),
        "conv_w": conv_w, "conv_b": vec(conv_b),
        "w_gate_lru": jnp.concatenate([block_diag(w_rg), block_diag(w_ig)], axis=2).astype(BF16),
        "b_rg": vec(b_rg), "b_ig": vec(b_ig), "lru_lambda": vec(lru_lambda),
        "w_out_ab": w_out_ab.astype(BF16),
        "w_in_c": w_in_c.astype(BF16), "ln_g_c": vec(ln_g_c), "ln_b_c": vec(ln_b_c),
        "w_s": w_s, "b_s": b_s, "w_out_c": w_out_c.astype(BF16),
        "w_gate": w_gate.astype(BF16), "w_up": w_up.astype(BF16), "w_down": w_down.astype(BF16),
        "w_pe": w_pe.astype(BF16), "w_pg": w_pg.astype(BF16),
    }


def _gmlp_spatial(prm, block_len):
    reps = CHUNK // block_len
    ws = jnp.tile(prm["w_s"][:, :, :block_len, :block_len], (1, 1, reps, reps))
    bs = jnp.tile(prm["b_s"][:, :, :block_len], (1, 1, reps))
    bs = jnp.repeat(jnp.swapaxes(bs, 1, 2), GMLP_GROUP_DIM, axis=2)
    return ws, bs


def _trunk(x, p, tables, n_pos_tiles, prm, past, batch, seq):
    n = batch * seq
    h = x.reshape(n, D_MODEL)
    p = p.reshape(DEPTH, n, PLE_DIM)
    block_len = min(seq, CHUNK)
    ws_tiled, bs_full = _gmlp_spatial(prm, block_len)
    ckv, kpe, lru, conv, vrows = [], [], [], [], []
    for i in range(DEPTH):
        j = i // 2
        if i % 2 == 0:
            if past is None:
                q, k, v, c_kv, k_pe, zx, zg = _in_even(h, tables, prm, j, n_pos_tiles, False)
                attn = _attn_prompt(q, k, v, batch, seq)
                lru_y, h_last, buf = _lru_prompt(zx, zg, prm, j, batch, seq)
                h_last = h_last.reshape(batch, LRU_WIDTH)
            else:
                qcat, kcat, c_kv, k_pe, zx, zg = _in_even(h, tables, prm, j, n_pos_tiles, True)
                q_s = qcat.reshape(MLA_HEADS, batch, seq, QK_CAT).transpose(1, 2, 0, 3)
                q_s = q_s.reshape(batch, seq * MLA_HEADS, QK_CAT)
                knew = jnp.pad(kcat.reshape(batch, seq, QK_CAT),
                               ((0, 0), (0, 2 * SUBLANES - seq), (0, 0)))
                o_lat = _attn_sample(q_s, knew, past["cache_ckv"], past["cache_kpe_t"],
                                     past["page_table"], j)
                attn = _uv_project(o_lat.reshape(n, MLA_HEADS * KV_LORA), prm, j)
                to_t = lambda a: a.reshape(batch, seq, LRU_WIDTH).transpose(1, 0, 2)
                y_t, h_last = _lru_sample(to_t(zx), to_t(zg),
                                          past["state_conv"][j].transpose(1, 0, 2),
                                          past["state_lru"][j], prm, j)
                lru_y = y_t.transpose(1, 0, 2).reshape(n, LRU_WIDTH)
                buf = jnp.concatenate([past["state_conv"][j], zx.reshape(batch, seq, LRU_WIDTH)],
                                      axis=1)[:, seq:]
            h = _post_ab(h, attn, lru_y, p, prm, i)
            ckv.append(c_kv.reshape(batch, seq, KV_LORA))
            kpe.append(k_pe.reshape(batch, seq, QK_ROPE))
            lru.append(h_last)
            conv.append(buf)
        else:
            h, v = _post_c(h, p, prm, i, ws_tiled, bs_full, block_len, past is not None)
            if v is not None:
                vrows.append(v.reshape(batch, seq, GMLP_WIDTH))
    return h.reshape(batch, seq, D_MODEL), ckv, kpe, lru, conv, vrows


def kernel(x_prompt, x_sample, cache_ckv, cache_kpe, state_lru, state_conv, page_table, p_prompt, p_sample, g_mix, g_ffn, g_pe, g_final, w_in_ab, g_qnorm, g_kvnorm, w_uq, w_uk, w_uv, conv_w, conv_b, w_rg, b_rg, w_ig, b_ig, lru_lambda, w_out_ab, w_in_c, ln_g_c, ln_b_c, w_s, b_s, w_out_c, w_gate, w_up, w_down, w_pe, w_pg):
    prm = _prepare_params(g_mix, g_ffn, g_pe, g_final, w_in_ab, g_qnorm, g_kvnorm, w_uq, w_uk,
                          w_uv, conv_w, conv_b, w_rg, b_rg, w_ig, b_ig, lru_lambda, w_out_ab,
                          w_in_c, ln_g_c, ln_b_c, w_s, b_s, w_out_c, w_gate, w_up, w_down, w_pe,
                          w_pg)
    batch, seq, _ = x_prompt.shape
    dec_batch, dec_seq, _ = x_sample.shape
    past_len = page_table.shape[1] * PAGE_SIZE

    tables_p = _rope_tables(jnp.arange(seq, dtype=jnp.int32))
    y_prompt, ckv_p, kpe_p, lru_p, conv_p, _ = _trunk(
        x_prompt, p_prompt, tables_p, seq // IN_TILE, prm, None, batch, seq)

    tables_s = _rope_tables(past_len + jnp.arange(dec_seq, dtype=jnp.int32), IN_TILE // dec_seq)
    past = {"cache_ckv": cache_ckv, "cache_kpe_t": jnp.swapaxes(cache_kpe, 2, 3), "state_lru": state_lru,
            "state_conv": state_conv, "page_table": page_table}
    y_sample, ckv_s, kpe_s, lru_s, conv_s, v_s = _trunk(
        x_sample, p_sample, tables_s, 1, prm, past, dec_batch, dec_seq)

    return (y_prompt, y_sample,
            jnp.stack(ckv_p), jnp.stack(kpe_p), jnp.stack(lru_p), jnp.stack(conv_p),
            jnp.stack(ckv_s), jnp.stack(kpe_s), jnp.stack(lru_s), jnp.stack(conv_s),
            jnp.stack(v_s))
```

```python
import functools
import math

import jax
import jax.numpy as jnp
from jax import lax
from jax.experimental import pallas as pl
from jax.experimental.pallas import tpu as pltpu

F32 = jnp.float32
BF16 = jnp.bfloat16

D_MODEL = 1024
DEPTH = 4
PAGE_SIZE = 128
MLA_HEADS = 8
Q_LORA = 384
KV_LORA = 256
QK_NOPE = 64
QK_ROPE = 32
ROPE_HALF = QK_ROPE // 2
V_DIM = 64
ROPE_THETA = 10000.0
ATTN_SCALE = 1.0 / math.sqrt(QK_NOPE + QK_ROPE)
LRU_WIDTH = 512
LRU_HEADS = 8
LRU_HEAD_DIM = LRU_WIDTH // LRU_HEADS
CONV_WIDTH = 4
LRU_C = 8.0
CHUNK = 128
GMLP_WIDTH = 1024
GMLP_GROUPS = 8
GMLP_GROUP_DIM = GMLP_WIDTH // GMLP_GROUPS
FFN_HIDDEN = -(-8 * D_MODEL // (3 * 256)) * 256
PLE_DIM = 256
EPS = 1e-6

LANES = 128
SUBLANES = 8
QK_CAT = KV_LORA + LANES
IN_AB_PAD = Q_LORA + KV_LORA + 2 * LRU_WIDTH + LANES
VMEM_LIMIT = 60 * 1024 * 1024

IN_TILE = 512
POST_AB_TILE = 512
POST_C_TILE = 512
ATTN_TILE = 256
LRU_TILE = 512
SAMPLE_KEY_SPANS = 4


def _const_spec(shape):
    nd = len(shape)
    return pl.BlockSpec(shape, lambda *_: (0,) * nd, pipeline_mode=pl.Buffered(1))


def _layer_spec(shape, layer):
    nd = len(shape)
    return pl.BlockSpec((None,) + tuple(shape), lambda *_: (layer,) + (0,) * nd,
                        pipeline_mode=pl.Buffered(1))


def _rms(x, g):
    return x * lax.rsqrt(jnp.mean(x * x, axis=-1, keepdims=True) + EPS) * g


def _dot(a, b):
    return jnp.dot(a, b, preferred_element_type=F32)


def _dot_nt(a, b):
    return lax.dot_general(a, b, (((1,), (1,)), ((), ())), preferred_element_type=F32)


def _rope_block(blk, cs, sn):
    partner = pltpu.roll(blk, LANES - ROPE_HALF, 1)
    return blk * cs + partner * sn


def _in_even_common(h_ref, csk_ref, snk_ref, gmix_ref, win_ref, gq_ref, wuq_ref, gkv_ref,
                    ckv_ref, kpe_ref, zx_ref, zg_ref):
    xn = _rms(h_ref[...], gmix_ref[...]).astype(BF16)
    z = _dot(xn, win_ref[...])
    o_kv = Q_LORA
    o_x = o_kv + KV_LORA
    o_g = o_x + LRU_WIDTH
    o_pe = o_g + LRU_WIDTH
    zx_ref[...] = z[:, o_x:o_g]
    zg_ref[...] = z[:, o_g:o_pe]
    ckv = _rms(z[:, o_kv:o_x], gkv_ref[...])
    kpe_blk = _rope_block(z[:, o_pe:o_pe + LANES], csk_ref[...], snk_ref[...])
    ckv_ref[...] = ckv
    kpe_ref[...] = kpe_blk[:, QK_NOPE:QK_NOPE + QK_ROPE]
    qn = _rms(z[:, :Q_LORA], gq_ref[...]).astype(BF16)
    q = _dot(qn, wuq_ref[...])
    return ckv, kpe_blk, q


def _in_even_sample_kernel(h_ref, csk_ref, snk_ref, gmix_ref, win_ref, gq_ref, wuq_ref, gkv_ref,
                           wuk_ref, qcat_ref, kcat_ref, ckv_ref, kpe_ref, zx_ref, zg_ref):
    ckv, kpe_blk, q = _in_even_common(h_ref, csk_ref, snk_ref, gmix_ref, win_ref, gq_ref, wuq_ref,
                                      gkv_ref, ckv_ref, kpe_ref, zx_ref, zg_ref)
    kcat_ref[:, :KV_LORA] = ckv.astype(BF16)
    kcat_ref[:, KV_LORA:] = kpe_blk.astype(BF16)
    for hh in range(MLA_HEADS):
        blk = q[:, hh * LANES:(hh + 1) * LANES]
        qcat_ref[hh, :, :KV_LORA] = _dot(blk[:, :QK_NOPE].astype(BF16), wuk_ref[hh]).astype(BF16)
        qcat_ref[hh, :, KV_LORA:] = _rope_block(blk, csk_ref[...], snk_ref[...]).astype(BF16)


def _in_even_prompt_kernel(h_ref, csk_ref, snk_ref, gmix_ref, win_ref, gq_ref, wuq_ref, gkv_ref,
                           csq_ref, snq_ref, wukt_ref, wuvc_ref,
                           q_ref, k_ref, v_ref, ckv_ref, kpe_ref, zx_ref, zg_ref):
    ckv, kpe_blk, q = _in_even_common(h_ref, csk_ref, snk_ref, gmix_ref, win_ref, gq_ref, wuq_ref,
                                      gkv_ref, ckv_ref, kpe_ref, zx_ref, zg_ref)
    cb = ckv.astype(BF16)
    k_nope = _dot(cb, wukt_ref[...])
    v = _dot(cb, wuvc_ref[...])
    ones_lane = (lax.broadcasted_iota(jnp.int32, (1, LANES), 1) == V_DIM).astype(F32)
    for hh in range(MLA_HEADS):
        blk = slice(hh * LANES, (hh + 1) * LANES)
        q_ref[hh] = _rope_block(q[:, blk], csq_ref[...], snq_ref[...]).astype(BF16)
        k_ref[hh] = (k_nope[:, blk] + kpe_blk).astype(BF16)
        v_ref[hh] = (v[:, blk] + ones_lane).astype(BF16)


def _in_even(h, tables, prm, j, n_pos_tiles, absorbed):
    n = h.shape[0]
    tm = IN_TILE
    row = lambda w: pl.BlockSpec((tm, w), lambda r: (r, 0))
    heads = lambda w: pl.BlockSpec((MLA_HEADS, tm, w), lambda r: (0, r, 0))
    pos = pl.BlockSpec((tm, LANES), lambda r: (r % n_pos_tiles, 0))
    in_specs = [
        row(D_MODEL), pos, pos,
        _layer_spec((1, D_MODEL), 2 * j),
        _layer_spec((D_MODEL, IN_AB_PAD), j),
        _layer_spec((1, Q_LORA), j),
        _layer_spec((Q_LORA, MLA_HEADS * LANES), j),
        _layer_spec((1, KV_LORA), j),
    ]
    args = [h, tables["csk"], tables["snk"], prm["g_mix"], prm["w_in_ab"], prm["g_qnorm"],
            prm["w_uq"], prm["g_kvnorm"]]
    tail_specs = (row(KV_LORA), row(QK_ROPE), row(LRU_WIDTH), row(LRU_WIDTH))
    tail_shapes = (
        jax.ShapeDtypeStruct((n, KV_LORA), F32),
        jax.ShapeDtypeStruct((n, QK_ROPE), F32),
        jax.ShapeDtypeStruct((n, LRU_WIDTH), F32),
        jax.ShapeDtypeStruct((n, LRU_WIDTH), F32),
    )
    if absorbed:
        body = _in_even_sample_kernel
        in_specs += [_layer_spec((MLA_HEADS, QK_NOPE, KV_LORA), j)]
        args += [prm["w_uk"]]
        out_specs = (heads(QK_CAT), row(QK_CAT)) + tail_specs
        out_shape = (jax.ShapeDtypeStruct((MLA_HEADS, n, QK_CAT), BF16),
                     jax.ShapeDtypeStruct((n, QK_CAT), BF16)) + tail_shapes
    else:
        body = _in_even_prompt_kernel
        in_specs += [pos, pos,
                     _layer_spec((KV_LORA, MLA_HEADS * LANES), j),
                     _layer_spec((KV_LORA, MLA_HEADS * LANES), j)]
        args += [tables["csq"], tables["snq"], prm["w_uk_t"], prm["w_uv_cat"]]
        out_specs = (heads(LANES), heads(LANES), heads(LANES)) + tail_specs
        out_shape = (jax.ShapeDtypeStruct((MLA_HEADS, n, LANES), BF16),) * 3 + tail_shapes
    return pl.pallas_call(
        body,
        grid=(n // tm,),
        in_specs=in_specs,
        out_specs=out_specs,
        out_shape=out_shape,
        compiler_params=pltpu.CompilerParams(dimension_semantics=("parallel",),
                                             vmem_limit_bytes=VMEM_LIMIT),
        name="in_even_absorbed" if absorbed else "in_even_heads",
    )(*args)


def _lru_gates(xc, wgate_ref, brg_ref, big_ref, lam_ref):
    g = _dot(xc.astype(BF16), wgate_ref[...])
    r = jax.nn.sigmoid(g[:, :LRU_WIDTH] + brg_ref[...])
    ig = jax.nn.sigmoid(g[:, LRU_WIDTH:] + big_ref[...])
    x = -lam_ref[...]
    softplus = jnp.maximum(x, 0.0) + jnp.log1p(jnp.exp(-jnp.abs(x)))
    log_a = -LRU_C * r * softplus
    a = jnp.exp(log_a)
    th = jnp.tanh(log_a)
    bx = jnp.sqrt(-2.0 * th / (1.0 - th)) * ig * xc
    return a, bx


def _shift_in_group(x, s, fill):
    g, _, w = x.shape
    return jnp.concatenate([jnp.full((g, s, w), fill, x.dtype), x[:, :-s, :]], axis=1)


def _lru_prompt_kernel(zx_ref, zg_ref, cw_ref, cb_ref, wgate_ref, brg_ref, big_ref, lam_ref,
                       y_ref, hlast_ref, conv_ref, xbuf, hprev):
    t = pl.program_id(1)
    tl = zx_ref.shape[0]
    pad = SUBLANES

    @pl.when(t == 0)
    def _():
        xbuf[:pad, :] = jnp.zeros((pad, LRU_WIDTH), F32)
        hprev[...] = jnp.zeros_like(hprev)

    x = zx_ref[...]
    xbuf[pad:, :] = x
    xc = cb_ref[...] + xbuf[pad - 3:pad - 3 + tl, :] * cw_ref[0:1, :]
    xc = xc + xbuf[pad - 2:pad - 2 + tl, :] * cw_ref[1:2, :]
    xc = xc + xbuf[pad - 1:pad - 1 + tl, :] * cw_ref[2:3, :]
    xc = xc + x * cw_ref[3:4, :]
    xbuf[:pad, :] = x[tl - pad:, :]

    a, b = _lru_gates(xc, wgate_ref, brg_ref, big_ref, lam_ref)
    groups = tl // SUBLANES
    a = a.reshape(groups, SUBLANES, LRU_WIDTH)
    b = b.reshape(groups, SUBLANES, LRU_WIDTH)
    s = 1
    while s < SUBLANES:
        b = a * _shift_in_group(b, s, 0.0) + b
        a = a * _shift_in_group(a, s, 1.0)
        s *= 2
    h = hprev[...]
    hs = []
    for g in range(groups):
        hs_g = a[g] * h + b[g]
        h = hs_g[SUBLANES - 1:, :]
        hs.append(hs_g)
    hs = jnp.concatenate(hs, axis=0)
    hprev[...] = h
    y_ref[...] = (hs * jax.nn.gelu(zg_ref[...])).astype(y_ref.dtype)

    @pl.when(t == pl.num_programs(1) - 1)
    def _():
        hlast_ref[...] = h
        conv_ref[...] = x[tl - (CONV_WIDTH - 1):, :]


def _lru_prompt(zx, zg, prm, j, batch, seq):
    tl = LRU_TILE
    nt = seq // tl
    row = pl.BlockSpec((tl, LRU_WIDTH), lambda b, t: (b * nt + t, 0))
    return pl.pallas_call(
        _lru_prompt_kernel,
        grid=(batch, nt),
        in_specs=[
            row, row,
            _layer_spec((CONV_WIDTH, LRU_WIDTH), j),
            _layer_spec((1, LRU_WIDTH), j),
            _layer_spec((LRU_WIDTH, 2 * LRU_WIDTH), j),
            _layer_spec((1, LRU_WIDTH), j),
            _layer_spec((1, LRU_WIDTH), j),
            _layer_spec((1, LRU_WIDTH), j),
        ],
        out_specs=(
            row,
            pl.BlockSpec((None, 1, LRU_WIDTH), lambda b, t: (b, 0, 0)),
            pl.BlockSpec((None, CONV_WIDTH - 1, LRU_WIDTH), lambda b, t: (b, 0, 0)),
        ),
        out_shape=(
            jax.ShapeDtypeStruct((batch * seq, LRU_WIDTH), BF16),
            jax.ShapeDtypeStruct((batch, 1, LRU_WIDTH), F32),
            jax.ShapeDtypeStruct((batch, CONV_WIDTH - 1, LRU_WIDTH), F32),
        ),
        scratch_shapes=[pltpu.VMEM((SUBLANES + tl, LRU_WIDTH), F32),
                        pltpu.VMEM((1, LRU_WIDTH), F32)],
        compiler_params=pltpu.CompilerParams(dimension_semantics=("parallel", "arbitrary"),
                                             vmem_limit_bytes=VMEM_LIMIT),
        name="lru_prompt",
    )(zx, zg, prm["conv_w"], prm["conv_b"], prm["w_gate_lru"], prm["b_rg"], prm["b_ig"],
      prm["lru_lambda"])


def _lru_sample_kernel(zx_ref, zg_ref, buf_ref, h0_ref, cw_ref, cb_ref, wgate_ref, brg_ref,
                       big_ref, lam_ref, y_ref, hlast_ref):
    nt = zx_ref.shape[0]
    nb = zx_ref.shape[1]
    xpad = [buf_ref[k] for k in range(CONV_WIDTH - 1)] + [zx_ref[k] for k in range(nt)]
    xcs = []
    for t in range(nt):
        xc = cb_ref[...] + xpad[t] * cw_ref[0:1, :]
        for k in range(1, CONV_WIDTH):
            xc = xc + xpad[t + k] * cw_ref[k:k + 1, :]
        xcs.append(xc)
    a, b = _lru_gates(jnp.concatenate(xcs, axis=0), wgate_ref, brg_ref, big_ref, lam_ref)
    h = h0_ref[...]
    for t in range(nt):
        h = a[t * nb:(t + 1) * nb] * h + b[t * nb:(t + 1) * nb]
        y_ref[t] = (h * jax.nn.gelu(zg_ref[t])).astype(y_ref.dtype)
    hlast_ref[...] = h


def _lru_sample(zx_t, zg_t, buf_t, h0, prm, j):
    nt, nb, _ = zx_t.shape
    return pl.pallas_call(
        _lru_sample_kernel,
        grid=(1,),
        in_specs=[
            _const_spec((nt, nb, LRU_WIDTH)), _const_spec((nt, nb, LRU_WIDTH)),
            _const_spec((CONV_WIDTH - 1, nb, LRU_WIDTH)), _const_spec((nb, LRU_WIDTH)),
            _layer_spec((CONV_WIDTH, LRU_WIDTH), j),
            _layer_spec((1, LRU_WIDTH), j),
            _layer_spec((LRU_WIDTH, 2 * LRU_WIDTH), j),
            _layer_spec((1, LRU_WIDTH), j),
            _layer_spec((1, LRU_WIDTH), j),
            _layer_spec((1, LRU_WIDTH), j),
        ],
        out_specs=(
            pl.BlockSpec((nt, nb, LRU_WIDTH), lambda i: (0, 0, 0)),
            pl.BlockSpec((nb, LRU_WIDTH), lambda i: (0, 0)),
        ),
        out_shape=(
            jax.ShapeDtypeStruct((nt, nb, LRU_WIDTH), BF16),
            jax.ShapeDtypeStruct((nb, LRU_WIDTH), F32),
        ),
        compiler_params=pltpu.CompilerParams(dimension_semantics=("arbitrary",),
                                             vmem_limit_bytes=VMEM_LIMIT),
        name="lru_sample",
    )(zx_t, zg_t, buf_t, h0, prm["conv_w"], prm["conv_b"], prm["w_gate_lru"], prm["b_rg"],
      prm["b_ig"], prm["lru_lambda"])


def _attn_prompt_kernel(q_ref, k_ref, v_ref, o_ref, m_sc, acc_sc):
    i = pl.program_id(1)
    tq = q_ref.shape[1]
    m_sc[...] = jnp.full(m_sc.shape, -jnp.inf, F32)
    acc_sc[...] = jnp.zeros(acc_sc.shape, F32)

    def key_block(start, width, masked):
        keys = pl.ds(start, width)
        if masked:
            q_pos = i * tq + lax.broadcasted_iota(jnp.int32, (tq, width), 0)
            causal = start + lax.broadcasted_iota(jnp.int32, (tq, width), 1) <= q_pos
        scores = [_dot_nt(q_ref[hh], k_ref[hh, keys, :]) for hh in range(MLA_HEADS)]
        for hh in range(MLA_HEADS):
            s = scores[hh]
            if masked:
                s = jnp.where(causal, s, -jnp.inf)
            m_prev = m_sc[hh]
            m_new = jnp.maximum(m_prev, jnp.max(s, axis=-1, keepdims=True))
            alpha = jnp.exp2(m_prev - m_new)
            p = jnp.exp2(s - jnp.concatenate([m_new] * (width // LANES), axis=1))
            acc_sc[hh] = alpha * acc_sc[hh] + _dot(p.astype(BF16), v_ref[hh, keys, :])
            m_sc[hh] = m_new

    def full_pair(jb, carry):
        key_block(pl.multiple_of(jb * (2 * tq), 2 * tq), 2 * tq, False)
        return carry

    lax.fori_loop(0, i // 2, full_pair, 0)

    @pl.when(i % 2 == 1)
    def _():
        key_block(pl.multiple_of((i - 1) * tq, 2 * tq), 2 * tq, True)

    @pl.when(i % 2 == 0)
    def _():
        key_block(pl.multiple_of(i * tq, tq), tq, True)

    outs = []
    for hh in range(MLA_HEADS):
        acc = acc_sc[hh]
        outs.append(acc[:, :V_DIM] / acc[:, V_DIM:V_DIM + 1])
    o_ref[...] = jnp.concatenate(outs, axis=-1).astype(o_ref.dtype)


def _attn_prompt(q, k, v, batch, seq):
    tq = ATTN_TILE
    nq = seq // tq
    return pl.pallas_call(
        _attn_prompt_kernel,
        grid=(batch, nq),
        in_specs=[
            pl.BlockSpec((MLA_HEADS, tq, LANES), lambda b, i: (0, b * nq + i, 0)),
            pl.BlockSpec((MLA_HEADS, seq, LANES), lambda b, i: (0, b, 0)),
            pl.BlockSpec((MLA_HEADS, seq, LANES), lambda b, i: (0, b, 0)),
        ],
        out_specs=pl.BlockSpec((tq, MLA_HEADS * V_DIM), lambda b, i: (b * nq + i, 0)),
        out_shape=jax.ShapeDtypeStruct((batch * seq, MLA_HEADS * V_DIM), BF16),
        scratch_shapes=[pltpu.VMEM((MLA_HEADS, tq, LANES), F32),
                        pltpu.VMEM((MLA_HEADS, tq, LANES), F32)],
        compiler_params=pltpu.CompilerParams(dimension_semantics=("parallel", "arbitrary"),
                                             vmem_limit_bytes=VMEM_LIMIT),
        name="attn_prompt",
    )(q, k, v)


def _attn_sample_kernel(pt_ref, q_ref, knew_ref, ckv_hbm, kpe_hbm, o_ref, cbuf, pbuf, sems, *,
                        layer, n_pages):
    b = pl.program_id(0)
    nb = pl.num_programs(0)
    slot = b % 2

    def page_copies(bb, sl):
        copies = []
        for pg in range(n_pages):
            page = pt_ref[bb * n_pages + pg]
            rows = pl.ds(pg * PAGE_SIZE, PAGE_SIZE)
            copies.append(pltpu.make_async_copy(ckv_hbm.at[layer, page], cbuf.at[sl, rows],
                                                sems.at[0, sl]))
            copies.append(pltpu.make_async_copy(kpe_hbm.at[layer, page], pbuf.at[sl, :, rows],
                                                sems.at[1, sl]))
        return copies

    @pl.when(b == 0)
    def _():
        for cp in page_copies(b, slot):
            cp.start()

    @pl.when(b + 1 < nb)
    def _():
        for cp in page_copies(b + 1, 1 - slot):
            cp.start()

    for cp in page_copies(b, slot):
        cp.wait()

    q = q_ref[...]
    q_lat = q[:, :KV_LORA]
    o_pe = KV_LORA + QK_NOPE
    q_pe = q[:, o_pe:o_pe + QK_ROPE]
    knew = knew_ref[...]
    s_new = _dot_nt(q, knew) * ATTN_SCALE
    t_row = lax.broadcasted_iota(jnp.int32, s_new.shape, 0) // MLA_HEADS
    t_key = lax.broadcasted_iota(jnp.int32, s_new.shape, 1)
    s_new = jnp.where(t_key <= t_row, s_new, -jnp.inf)
    span = cbuf.shape[1] // SAMPLE_KEY_SPANS
    spans = [slice(c * span, (c + 1) * span) for c in range(SAMPLE_KEY_SPANS)]
    kcs = [cbuf[slot, sp, :].astype(BF16) for sp in spans]
    s_lat = [_dot_nt(q_lat, kc) for kc in kcs]
    s_pe = [_dot(q_pe, pbuf[slot, :, sp].astype(BF16)) for sp in spans]
    scores = [(a + b) * ATTN_SCALE for a, b in zip(s_lat, s_pe)]
    m = jnp.max(s_new, axis=-1, keepdims=True)
    for s in scores:
        m = jnp.maximum(m, jnp.max(s, axis=-1, keepdims=True))
    p_new = jnp.exp(s_new - m)
    denom = jnp.sum(p_new, axis=-1, keepdims=True)
    outs = [_dot(p_new.astype(BF16), knew[:, :KV_LORA])]
    for s, kc in zip(scores, kcs):
        p = jnp.exp(s - m)
        denom = denom + jnp.sum(p, axis=-1, keepdims=True)
        outs.append(_dot(p.astype(BF16), kc))
    while len(outs) > 1:
        outs = [outs[k] + outs[k + 1] if k + 1 < len(outs) else outs[k]
                for k in range(0, len(outs), 2)]
    o_ref[...] = (outs[0] / denom).astype(o_ref.dtype)


def _attn_sample(q_s, knew, cache_ckv, cache_kpe_t, page_table, j):
    nb, rows, _ = q_s.shape
    n_pages = page_table.shape[1]
    past = n_pages * PAGE_SIZE
    kernel = functools.partial(_attn_sample_kernel, layer=j, n_pages=n_pages)
    grid_spec = pltpu.PrefetchScalarGridSpec(
        num_scalar_prefetch=1,
        grid=(nb,),
        in_specs=[
            pl.BlockSpec((None, rows, QK_CAT), lambda b, pt: (b, 0, 0)),
            pl.BlockSpec((None, knew.shape[1], QK_CAT), lambda b, pt: (b, 0, 0)),
            pl.BlockSpec(memory_space=pl.ANY),
            pl.BlockSpec(memory_space=pl.ANY),
        ],
        out_specs=pl.BlockSpec((None, rows, KV_LORA), lambda b, pt: (b, 0, 0)),
        scratch_shapes=[
            pltpu.VMEM((2, past, KV_LORA), F32),
            pltpu.VMEM((2, QK_ROPE, past), F32),
            pltpu.SemaphoreType.DMA((2, 2)),
        ],
    )
    return pl.pallas_call(
        kernel,
        grid_spec=grid_spec,
        out_shape=jax.ShapeDtypeStruct((nb, rows, KV_LORA), BF16),
        compiler_params=pltpu.CompilerParams(dimension_semantics=("arbitrary",),
                                             vmem_limit_bytes=VMEM_LIMIT),
        name="attn_sample",
    )(page_table.reshape(-1), q_s, knew, cache_ckv, cache_kpe_t)


def _uv_kernel(o_ref, wuv_ref, a_ref):
    outs = [_dot(o_ref[:, hh * KV_LORA:(hh + 1) * KV_LORA], wuv_ref[hh])
            for hh in range(MLA_HEADS)]
    a_ref[...] = jnp.concatenate(outs, axis=-1).astype(a_ref.dtype)


def _uv_project(o_lat, prm, j):
    n = o_lat.shape[0]
    return pl.pallas_call(
        _uv_kernel,
        grid=(1,),
        in_specs=[_const_spec((n, MLA_HEADS * KV_LORA)),
                  _layer_spec((MLA_HEADS, KV_LORA, V_DIM), j)],
        out_specs=pl.BlockSpec((n, MLA_HEADS * V_DIM), lambda i: (0, 0)),
        out_shape=jax.ShapeDtypeStruct((n, MLA_HEADS * V_DIM), BF16),
        compiler_params=pltpu.CompilerParams(dimension_semantics=("arbitrary",),
                                             vmem_limit_bytes=VMEM_LIMIT),
        name="uv_project",
    )(o_lat, prm["w_uv"])


def _ffn_and_gate(h1, p_ref, gffn_ref, wg_ref, wu_ref, wd_ref, gpe_ref, wpg_ref, wpe_ref):
    hn = _rms(h1, gffn_ref[...]).astype(BF16)
    act = (jax.nn.silu(_dot(hn, wg_ref[...])) * _dot(hn, wu_ref[...])).astype(BF16)
    h2 = h1 + _dot(act, wd_ref[...])
    gate = jax.nn.sigmoid(_dot(_rms(h2, gpe_ref[...]).astype(BF16), wpg_ref[...]))
    return h2 + _dot(p_ref[...].astype(BF16), wpe_ref[...]) * gate


def _post_ab_kernel(h_ref, attn_ref, lru_ref, p_ref, wout_ref, gffn_ref, wg_ref, wu_ref, wd_ref,
                    gpe_ref, wpg_ref, wpe_ref, out_ref):
    n_attn = MLA_HEADS * V_DIM
    mix = _dot(attn_ref[...], wout_ref[:n_attn, :]) + _dot(lru_ref[...], wout_ref[n_attn:, :])
    h1 = h_ref[...] + mix
    out_ref[...] = _ffn_and_gate(h1, p_ref, gffn_ref, wg_ref, wu_ref, wd_ref, gpe_ref, wpg_ref,
                                 wpe_ref)


def _post_c_kernel(h_ref, p_ref, gmix_ref, winc_ref, lng_ref, lnb_ref, ws_ref, bs_ref, woutc_ref,
                   gffn_ref, wg_ref, wu_ref, wd_ref, gpe_ref, wpg_ref, wpe_ref, gfin_ref,
                   out_ref, *rest, block_len, final):
    sp_sc = rest[-1]
    h = h_ref[...]
    tm = h.shape[0]
    z = jax.nn.gelu(_dot(_rms(h, gmix_ref[...]).astype(BF16), winc_ref[...]))
    u = z[:, :GMLP_WIDTH]
    v = z[:, GMLP_WIDTH:]
    vc = v - jnp.mean(v, axis=-1, keepdims=True)
    v = vc * lax.rsqrt(jnp.mean(vc * vc, axis=-1, keepdims=True) + EPS) * lng_ref[...] + lnb_ref[...]
    if len(rest) == 2:
        rest[0][...] = v
    vb = v.astype(BF16)
    t_idx = lax.broadcasted_iota(jnp.int32, (CHUNK, CHUNK), 0)
    s_idx = lax.broadcasted_iota(jnp.int32, (CHUNK, CHUNK), 1)
    mask = (s_idx <= t_idx) & ((s_idx // block_len) == (t_idx // block_len))
    for gg in range(GMLP_GROUPS):
        wm = jnp.where(mask, ws_ref[gg], 0.0).astype(BF16)
        cols = slice(gg * GMLP_GROUP_DIM, (gg + 1) * GMLP_GROUP_DIM)
        for cc in range(tm // CHUNK):
            rows = slice(cc * CHUNK, (cc + 1) * CHUNK)
            sp_sc[rows, cols] = _dot(wm, vb[rows, cols]) + bs_ref[:, cols]
    mix = _dot((u * sp_sc[...]).astype(BF16), woutc_ref[...])
    h3 = _ffn_and_gate(h + mix, p_ref, gffn_ref, wg_ref, wu_ref, wd_ref, gpe_ref, wpg_ref, wpe_ref)
    out_ref[...] = _rms(h3, gfin_ref[...]) if final else h3


def _ffn_specs(i):
    return [
        _layer_spec((1, D_MODEL), i),
        _layer_spec((D_MODEL, FFN_HIDDEN), i),
        _layer_spec((D_MODEL, FFN_HIDDEN), i),
        _layer_spec((FFN_HIDDEN, D_MODEL), i),
        _layer_spec((1, D_MODEL), i),
        _layer_spec((D_MODEL, D_MODEL), i),
        _layer_spec((PLE_DIM, D_MODEL), i),
    ]


def _ffn_args(prm):
    return (prm["g_ffn"], prm["w_gate"], prm["w_up"], prm["w_down"], prm["g_pe"], prm["w_pg"],
            prm["w_pe"])


def _post_ab(h, attn, lru_y, p, prm, i):
    n = h.shape[0]
    tm = POST_AB_TILE
    j = i // 2
    row = lambda w: pl.BlockSpec((tm, w), lambda r: (r, 0))
    return pl.pallas_call(
        _post_ab_kernel,
        grid=(n // tm,),
        in_specs=[
            row(D_MODEL), row(MLA_HEADS * V_DIM), row(LRU_WIDTH),
            pl.BlockSpec((None, tm, PLE_DIM), lambda r: (i, r, 0)),
            _layer_spec((MLA_HEADS * V_DIM + LRU_WIDTH, D_MODEL), j),
        ] + _ffn_specs(i),
        out_specs=row(D_MODEL),
        out_shape=jax.ShapeDtypeStruct((n, D_MODEL), F32),
        compiler_params=pltpu.CompilerParams(dimension_semantics=("parallel",),
                                             vmem_limit_bytes=VMEM_LIMIT),
        name="post_ab",
    )(h, attn, lru_y, p, prm["w_out_ab"], *_ffn_args(prm))


def _post_c(h, p, prm, i, ws_tiled, bs_full, block_len, emit_v):
    n = h.shape[0]
    tm = POST_C_TILE
    j = i // 2
    final = i == DEPTH - 1
    row = lambda w: pl.BlockSpec((tm, w), lambda r: (r, 0))
    kernel = functools.partial(_post_c_kernel, block_len=block_len, final=final)
    out_specs = [row(D_MODEL)] + ([row(GMLP_WIDTH)] if emit_v else [])
    out_shape = [jax.ShapeDtypeStruct((n, D_MODEL), F32)]
    if emit_v:
        out_shape.append(jax.ShapeDtypeStruct((n, GMLP_WIDTH), F32))
    outs = pl.pallas_call(
        kernel,
        grid=(n // tm,),
        in_specs=[
            row(D_MODEL),
            pl.BlockSpec((None, tm, PLE_DIM), lambda r: (i, r, 0)),
            _layer_spec((1, D_MODEL), i),
            _layer_spec((D_MODEL, 2 * GMLP_WIDTH), j),
            _layer_spec((1, GMLP_WIDTH), j),
            _layer_spec((1, GMLP_WIDTH), j),
            _layer_spec((GMLP_GROUPS, CHUNK, CHUNK), j),
            _layer_spec((CHUNK, GMLP_WIDTH), j),
            _layer_spec((GMLP_WIDTH, D_MODEL), j),
        ] + _ffn_specs(i) + [_const_spec((1, D_MODEL))],
        out_specs=out_specs,
        out_shape=out_shape,
        scratch_shapes=[pltpu.VMEM((tm, GMLP_WIDTH), F32)],
        compiler_params=pltpu.CompilerParams(dimension_semantics=("parallel",),
                                             vmem_limit_bytes=VMEM_LIMIT),
        name="post_c",
    )(h, p, prm["g_mix"], prm["w_in_c"], prm["ln_g_c"], prm["ln_b_c"], ws_tiled, bs_full,
      prm["w_out_c"], *_ffn_args(prm), prm["g_final"])
    return outs[0], (outs[1] if emit_v else None)


def _rope_tables(pos, reps=1):
    inv = jnp.exp(-math.log(ROPE_THETA) * jnp.arange(ROPE_HALF, dtype=F32) / ROPE_HALF)
    ang = pos.astype(F32)[:, None] * inv[None, :]
    cos, sin = jnp.cos(ang), jnp.sin(ang)
    t = pos.shape[0]
    head = jnp.zeros((t, QK_NOPE), F32)
    tail = jnp.zeros((t, LANES - QK_NOPE - QK_ROPE), F32)
    csk = jnp.concatenate([head, cos, cos, tail], axis=1)
    snk = jnp.concatenate([head, -sin, sin, tail], axis=1)
    log2_scale = ATTN_SCALE * math.log2(math.e)
    csq = jnp.concatenate([head + 1.0, cos, cos, tail], axis=1) * log2_scale
    tables = {"csk": csk, "snk": snk, "csq": csq, "snq": snk * log2_scale}
    return {name: jnp.tile(tab, (reps, 1)) for name, tab in tables.items()}


def _prepare_params(g_mix, g_ffn, g_pe, g_final, w_in_ab, g_qnorm, g_kvnorm, w_uq, w_uk, w_uv,
                    conv_w, conv_b, w_rg, b_rg, w_ig, b_ig, lru_lambda, w_out_ab, w_in_c, ln_g_c,
                    ln_b_c, w_s, b_s, w_out_c, w_gate, w_up, w_down, w_pe, w_pg):
    n_ab = w_in_ab.shape[0]
    o1 = Q_LORA + KV_LORA
    o2 = o1 + QK_ROPE
    pad_tail = LANES - QK_NOPE - QK_ROPE - ROPE_HALF
    w_in = jnp.concatenate(
        [w_in_ab[:, :, :o1], w_in_ab[:, :, o2:], jnp.zeros((n_ab, D_MODEL, QK_NOPE), F32),
         w_in_ab[:, :, o1:o2], w_in_ab[:, :, o1:o1 + ROPE_HALF],
         jnp.zeros((n_ab, D_MODEL, pad_tail), F32)], axis=2)
    uq = w_uq.reshape(n_ab, Q_LORA, MLA_HEADS, QK_NOPE + QK_ROPE)
    uq = jnp.concatenate([uq, uq[..., QK_NOPE:QK_NOPE + ROPE_HALF],
                          jnp.zeros(uq.shape[:3] + (pad_tail,), F32)], axis=3)
    uq = uq.reshape(n_ab, Q_LORA, MLA_HEADS * LANES)
    uk_t = jnp.pad(w_uk.transpose(0, 3, 1, 2), ((0, 0), (0, 0), (0, 0), (0, LANES - QK_NOPE)))
    uk_t = uk_t.reshape(n_ab, KV_LORA, MLA_HEADS * LANES)
    uv_cat = jnp.pad(w_uv.transpose(0, 2, 1, 3), ((0, 0), (0, 0), (0, 0), (0, LANES - V_DIM)))
    uv_cat = uv_cat.reshape(n_ab, KV_LORA, MLA_HEADS * LANES)

    def block_diag(w):
        eye = jnp.eye(LRU_HEADS, dtype=F32)
        return jnp.einsum("jhab,hg->jhagb", w, eye).reshape(n_ab, LRU_WIDTH, LRU_WIDTH)

    vec = lambda a: a[:, None, :]
    return {
        "g_mix": vec(g_mix), "g_ffn": vec(g_ffn), "g_pe": vec(g_pe), "g_final": g_final[None, :],
        "w_in_ab": w_in.astype(BF16), "g_qnorm": vec(g_qnorm), "g_kvnorm": vec(g_kvnorm),
        "w_uq": uq.astype(BF16),
        "w_uk": w_uk.astype(BF16), "w_uv": w_uv.astype(BF16),
        "w_uk_t": uk_t.astype(BF16), "w_uv_cat": uv_cat.astype(BF16),
        "conv_w": conv_w, "conv_b": vec(conv_b),
        "w_gate_lru": jnp.concatenate([block_diag(w_rg), block_diag(w_ig)], axis=2).astype(BF16),
        "b_rg": vec(b_rg), "b_ig": vec(b_ig), "lru_lambda": vec(lru_lambda),
        "w_out_ab": w_out_ab.astype(BF16),
        "w_in_c": w_in_c.astype(BF16), "ln_g_c": vec(ln_g_c), "ln_b_c": vec(ln_b_c),
        "w_s": w_s, "b_s": b_s, "w_out_c": w_out_c.astype(BF16),
        "w_gate": w_gate.astype(BF16), "w_up": w_up.astype(BF16), "w_down": w_down.astype(BF16),
        "w_pe": w_pe.astype(BF16), "w_pg": w_pg.astype(BF16),
    }


def _gmlp_spatial(prm, block_len):
    reps = CHUNK // block_len
    ws = jnp.tile(prm["w_s"][:, :, :block_len, :block_len], (1, 1, reps, reps))
    bs = jnp.tile(prm["b_s"][:, :, :block_len], (1, 1, reps))
    bs = jnp.repeat(jnp.swapaxes(bs, 1, 2), GMLP_GROUP_DIM, axis=2)
    return ws, bs


def _trunk(x, p, tables, n_pos_tiles, prm, past, batch, seq):
    n = batch * seq
    h = x.reshape(n, D_MODEL)
    p = p.reshape(DEPTH, n, PLE_DIM)
    block_len = min(seq, CHUNK)
    ws_tiled, bs_full = _gmlp_spatial(prm, block_len)
    ckv, kpe, lru, conv, vrows = [], [], [], [], []
    for i in range(DEPTH):
        j = i // 2
        if i % 2 == 0:
            if past is None:
                q, k, v, c_kv, k_pe, zx, zg = _in_even(h, tables, prm, j, n_pos_tiles, False)
                attn = _attn_prompt(q, k, v, batch, seq)
                lru_y, h_last, buf = _lru_prompt(zx, zg, prm, j, batch, seq)
                h_last = h_last.reshape(batch, LRU_WIDTH)
            else:
                qcat, kcat, c_kv, k_pe, zx, zg = _in_even(h, tables, prm, j, n_pos_tiles, True)
                q_s = qcat.reshape(MLA_HEADS, batch, seq, QK_CAT).transpose(1, 2, 0, 3)
                q_s = q_s.reshape(batch, seq * MLA_HEADS, QK_CAT)
                knew = jnp.pad(kcat.reshape(batch, seq, QK_CAT),
                               ((0, 0), (0, 2 * SUBLANES - seq), (0, 0)))
                o_lat = _attn_sample(q_s, knew, past["cache_ckv"], past["cache_kpe_t"],
                                     past["page_table"], j)
                attn = _uv_project(o_lat.reshape(n, MLA_HEADS * KV_LORA), prm, j)
                to_t = lambda a: a.reshape(batch, seq, LRU_WIDTH).transpose(1, 0, 2)
                y_t, h_last = _lru_sample(to_t(zx), to_t(zg),
                                          past["state_conv"][j].transpose(1, 0, 2),
                                          past["state_lru"][j], prm, j)
                lru_y = y_t.transpose(1, 0, 2).reshape(n, LRU_WIDTH)
                buf = jnp.concatenate([past["state_conv"][j], zx.reshape(batch, seq, LRU_WIDTH)],
                                      axis=1)[:, seq:]
            h = _post_ab(h, attn, lru_y, p, prm, i)
            ckv.append(c_kv.reshape(batch, seq, KV_LORA))
            kpe.append(k_pe.reshape(batch, seq, QK_ROPE))
            lru.append(h_last)
            conv.append(buf)
        else:
            h, v = _post_c(h, p, prm, i, ws_tiled, bs_full, block_len, past is not None)
            if v is not None:
                vrows.append(v.reshape(batch, seq, GMLP_WIDTH))
    return h.reshape(batch, seq, D_MODEL), ckv, kpe, lru, conv, vrows


def kernel(x_prompt, x_sample, cache_ckv, cache_kpe, state_lru, state_conv, page_table, p_prompt, p_sample, g_mix, g_ffn, g_pe, g_final, w_in_ab, g_qnorm, g_kvnorm, w_uq, w_uk, w_uv, conv_w, conv_b, w_rg, b_rg, w_ig, b_ig, lru_lambda, w_out_ab, w_in_c, ln_g_c, ln_b_c, w_s, b_s, w_out_c, w_gate, w_up, w_down, w_pe, w_pg):
    prm = _prepare_params(g_mix, g_ffn, g_pe, g_final, w_in_ab, g_qnorm, g_kvnorm, w_uq, w_uk,
                          w_uv, conv_w, conv_b, w_rg, b_rg, w_ig, b_ig, lru_lambda, w_out_ab,
                          w_in_c, ln_g_c, ln_b_c, w_s, b_s, w_out_c, w_gate, w_up, w_down, w_pe,
                          w_pg)
    batch, seq, _ = x_prompt.shape
    dec_batch, dec_seq, _ = x_sample.shape
    past_len = page_table.shape[1] * PAGE_SIZE

    tables_p = _rope_tables(jnp.arange(seq, dtype=jnp.int32))
    y_prompt, ckv_p, kpe_p, lru_p, conv_p, _ = _trunk(
        x_prompt, p_prompt, tables_p, seq // IN_TILE, prm, None, batch, seq)

    tables_s = _rope_tables(past_len + jnp.arange(dec_seq, dtype=jnp.int32), IN_TILE // dec_seq)
    past = {"cache_ckv": cache_ckv, "cache_kpe_t": jnp.swapaxes(cache_kpe, 2, 3), "state_lru": state_lru,
            "state_conv": state_conv, "page_table": page_table}
    y_sample, ckv_s, kpe_s, lru_s, conv_s, v_s = _trunk(
        x_sample, p_sample, tables_s, 1, prm, past, dec_batch, dec_seq)

    return (y_prompt, y_sample,
            jnp.stack(ckv_p), jnp.stack(kpe_p), jnp.stack(lru_p), jnp.stack(conv_p),
            jnp.stack(ckv_s), jnp.stack(kpe_s), jnp.stack(lru_s), jnp.stack(conv_s),
            jnp.stack(v_s))
```

```python
import functools
import math

import jax
import jax.numpy as jnp
from jax import lax
from jax.experimental import pallas as pl
from jax.experimental.pallas import tpu as pltpu

F32 = jnp.float32
BF16 = jnp.bfloat16

D_MODEL = 1024
DEPTH = 4
PAGE_SIZE = 128
MLA_HEADS = 8
Q_LORA = 384
KV_LORA = 256
QK_NOPE = 64
QK_ROPE = 32
ROPE_HALF = QK_ROPE // 2
V_DIM = 64
ROPE_THETA = 10000.0
ATTN_SCALE = 1.0 / math.sqrt(QK_NOPE + QK_ROPE)
LRU_WIDTH = 512
LRU_HEADS = 8
LRU_HEAD_DIM = LRU_WIDTH // LRU_HEADS
CONV_WIDTH = 4
LRU_C = 8.0
CHUNK = 128
GMLP_WIDTH = 1024
GMLP_GROUPS = 8
GMLP_GROUP_DIM = GMLP_WIDTH // GMLP_GROUPS
FFN_HIDDEN = -(-8 * D_MODEL // (3 * 256)) * 256
PLE_DIM = 256
EPS = 1e-6

LANES = 128
SUBLANES = 8
QK_CAT = KV_LORA + LANES
IN_AB_PAD = Q_LORA + KV_LORA + 2 * LRU_WIDTH + LANES
VMEM_LIMIT = 60 * 1024 * 1024

IN_TILE = 512
POST_AB_TILE = 512
POST_C_TILE = 512
ATTN_TILE = 512
LRU_TILE = 512
SAMPLE_KEY_SPANS = 4


def _const_spec(shape):
    nd = len(shape)
    return pl.BlockSpec(shape, lambda *_: (0,) * nd, pipeline_mode=pl.Buffered(1))


def _layer_spec(shape, layer):
    nd = len(shape)
    return pl.BlockSpec((None,) + tuple(shape), lambda *_: (layer,) + (0,) * nd,
                        pipeline_mode=pl.Buffered(1))


def _rms(x, g):
    return x * lax.rsqrt(jnp.mean(x * x, axis=-1, keepdims=True) + EPS) * g


def _dot(a, b):
    return jnp.dot(a, b, preferred_element_type=F32)


def _dot_nt(a, b):
    return lax.dot_general(a, b, (((1,), (1,)), ((), ())), preferred_element_type=F32)


def _rope_block(blk, cs, sn):
    partner = pltpu.roll(blk, LANES - ROPE_HALF, 1)
    return blk * cs + partner * sn


def _in_even_common(h_ref, csk_ref, snk_ref, gmix_ref, win_ref, gq_ref, wuq_ref, gkv_ref,
                    ckv_ref, kpe_ref, zx_ref, zg_ref):
    xn = _rms(h_ref[...], gmix_ref[...]).astype(BF16)
    z = _dot(xn, win_ref[...])
    o_kv = Q_LORA
    o_x = o_kv + KV_LORA
    o_g = o_x + LRU_WIDTH
    o_pe = o_g + LRU_WIDTH
    zx_ref[...] = z[:, o_x:o_g]
    zg_ref[...] = z[:, o_g:o_pe]
    ckv = _rms(z[:, o_kv:o_x], gkv_ref[...])
    kpe_blk = _rope_block(z[:, o_pe:o_pe + LANES], csk_ref[...], snk_ref[...])
    ckv_ref[...] = ckv
    kpe_ref[...] = kpe_blk[:, QK_NOPE:QK_NOPE + QK_ROPE]
    qn = _rms(z[:, :Q_LORA], gq_ref[...]).astype(BF16)
    q = _dot(qn, wuq_ref[...])
    return ckv, kpe_blk, q


def _in_even_sample_kernel(h_ref, csk_ref, snk_ref, gmix_ref, win_ref, gq_ref, wuq_ref, gkv_ref,
                           wuk_ref, qcat_ref, kcat_ref, ckv_ref, kpe_ref, zx_ref, zg_ref):
    ckv, kpe_blk, q = _in_even_common(h_ref, csk_ref, snk_ref, gmix_ref, win_ref, gq_ref, wuq_ref,
                                      gkv_ref, ckv_ref, kpe_ref, zx_ref, zg_ref)
    kcat_ref[:, :KV_LORA] = ckv.astype(BF16)
    kcat_ref[:, KV_LORA:] = kpe_blk.astype(BF16)
    for hh in range(MLA_HEADS):
        blk = q[:, hh * LANES:(hh + 1) * LANES]
        qcat_ref[hh, :, :KV_LORA] = _dot(blk[:, :QK_NOPE].astype(BF16), wuk_ref[hh]).astype(BF16)
        qcat_ref[hh, :, KV_LORA:] = _rope_block(blk, csk_ref[...], snk_ref[...]).astype(BF16)


def _in_even_prompt_kernel(h_ref, csk_ref, snk_ref, gmix_ref, win_ref, gq_ref, wuq_ref, gkv_ref,
                           csq_ref, snq_ref, wukt_ref, wuvc_ref,
                           q_ref, k_ref, v_ref, ckv_ref, kpe_ref, zx_ref, zg_ref):
    ckv, kpe_blk, q = _in_even_common(h_ref, csk_ref, snk_ref, gmix_ref, win_ref, gq_ref, wuq_ref,
                                      gkv_ref, ckv_ref, kpe_ref, zx_ref, zg_ref)
    cb = ckv.astype(BF16)
    k_nope = _dot(cb, wukt_ref[...])
    v = _dot(cb, wuvc_ref[...])
    ones_lane = (lax.broadcasted_iota(jnp.int32, (1, LANES), 1) == V_DIM).astype(F32)
    for hh in range(MLA_HEADS):
        blk = slice(hh * LANES, (hh + 1) * LANES)
        q_ref[hh] = _rope_block(q[:, blk], csq_ref[...], snq_ref[...]).astype(BF16)
        k_ref[hh] = (k_nope[:, blk] + kpe_blk).astype(BF16)
        v_ref[hh] = (v[:, blk] + ones_lane).astype(BF16)


def _in_even(h, tables, prm, j, n_pos_tiles, absorbed):
    n = h.shape[0]
    tm = IN_TILE
    row = lambda w: pl.BlockSpec((tm, w), lambda r: (r, 0))
    heads = lambda w: pl.BlockSpec((MLA_HEADS, tm, w), lambda r: (0, r, 0))
    pos = pl.BlockSpec((tm, LANES), lambda r: (r % n_pos_tiles, 0))
    in_specs = [
        row(D_MODEL), pos, pos,
        _layer_spec((1, D_MODEL), 2 * j),
        _layer_spec((D_MODEL, IN_AB_PAD), j),
        _layer_spec((1, Q_LORA), j),
        _layer_spec((Q_LORA, MLA_HEADS * LANES), j),
        _layer_spec((1, KV_LORA), j),
    ]
    args = [h, tables["csk"], tables["snk"], prm["g_mix"], prm["w_in_ab"], prm["g_qnorm"],
            prm["w_uq"], prm["g_kvnorm"]]
    tail_specs = (row(KV_LORA), row(QK_ROPE), row(LRU_WIDTH), row(LRU_WIDTH))
    tail_shapes = (
        jax.ShapeDtypeStruct((n, KV_LORA), F32),
        jax.ShapeDtypeStruct((n, QK_ROPE), F32),
        jax.ShapeDtypeStruct((n, LRU_WIDTH), F32),
        jax.ShapeDtypeStruct((n, LRU_WIDTH), F32),
    )
    if absorbed:
        body = _in_even_sample_kernel
        in_specs += [_layer_spec((MLA_HEADS, QK_NOPE, KV_LORA), j)]
        args += [prm["w_uk"]]
        out_specs = (heads(QK_CAT), row(QK_CAT)) + tail_specs
        out_shape = (jax.ShapeDtypeStruct((MLA_HEADS, n, QK_CAT), BF16),
                     jax.ShapeDtypeStruct((n, QK_CAT), BF16)) + tail_shapes
    else:
        body = _in_even_prompt_kernel
        in_specs += [pos, pos,
                     _layer_spec((KV_LORA, MLA_HEADS * LANES), j),
                     _layer_spec((KV_LORA, MLA_HEADS * LANES), j)]
        args += [tables["csq"], tables["snq"], prm["w_uk_t"], prm["w_uv_cat"]]
        out_specs = (heads(LANES), heads(LANES), heads(LANES)) + tail_specs
        out_shape = (jax.ShapeDtypeStruct((MLA_HEADS, n, LANES), BF16),) * 3 + tail_shapes
    return pl.pallas_call(
        body,
        grid=(n // tm,),
        in_specs=in_specs,
        out_specs=out_specs,
        out_shape=out_shape,
        compiler_params=pltpu.CompilerParams(dimension_semantics=("parallel",),
                                             vmem_limit_bytes=VMEM_LIMIT),
        name="in_even_absorbed" if absorbed else "in_even_heads",
    )(*args)


def _lru_gates(xc, wgate_ref, brg_ref, big_ref, lam_ref):
    g = _dot(xc.astype(BF16), wgate_ref[...])
    r = jax.nn.sigmoid(g[:, :LRU_WIDTH] + brg_ref[...])
    ig = jax.nn.sigmoid(g[:, LRU_WIDTH:] + big_ref[...])
    x = -lam_ref[...]
    softplus = jnp.maximum(x, 0.0) + jnp.log1p(jnp.exp(-jnp.abs(x)))
    log_a = -LRU_C * r * softplus
    a = jnp.exp(log_a)
    th = jnp.tanh(log_a)
    bx = jnp.sqrt(-2.0 * th / (1.0 - th)) * ig * xc
    return a, bx


def _shift_in_group(x, s, fill):
    g, _, w = x.shape
    return jnp.concatenate([jnp.full((g, s, w), fill, x.dtype), x[:, :-s, :]], axis=1)


def _lru_prompt_kernel(zx_ref, zg_ref, cw_ref, cb_ref, wgate_ref, brg_ref, big_ref, lam_ref,
                       y_ref, hlast_ref, conv_ref, xbuf, hprev):
    t = pl.program_id(1)
    tl = zx_ref.shape[0]
    pad = SUBLANES

    @pl.when(t == 0)
    def _():
        xbuf[:pad, :] = jnp.zeros((pad, LRU_WIDTH), F32)
        hprev[...] = jnp.zeros_like(hprev)

    x = zx_ref[...]
    xbuf[pad:, :] = x
    xc = cb_ref[...] + xbuf[pad - 3:pad - 3 + tl, :] * cw_ref[0:1, :]
    xc = xc + xbuf[pad - 2:pad - 2 + tl, :] * cw_ref[1:2, :]
    xc = xc + xbuf[pad - 1:pad - 1 + tl, :] * cw_ref[2:3, :]
    xc = xc + x * cw_ref[3:4, :]
    xbuf[:pad, :] = x[tl - pad:, :]

    a, b = _lru_gates(xc, wgate_ref, brg_ref, big_ref, lam_ref)
    groups = tl // SUBLANES
    a = a.reshape(groups, SUBLANES, LRU_WIDTH)
    b = b.reshape(groups, SUBLANES, LRU_WIDTH)
    s = 1
    while s < SUBLANES:
        b = a * _shift_in_group(b, s, 0.0) + b
        a = a * _shift_in_group(a, s, 1.0)
        s *= 2
    h = hprev[...]
    hs = []
    for g in range(groups):
        hs_g = a[g] * h + b[g]
        h = hs_g[SUBLANES - 1:, :]
        hs.append(hs_g)
    hs = jnp.concatenate(hs, axis=0)
    hprev[...] = h
    y_ref[...] = (hs * jax.nn.gelu(zg_ref[...])).astype(y_ref.dtype)

    @pl.when(t == pl.num_programs(1) - 1)
    def _():
        hlast_ref[...] = h
        conv_ref[...] = x[tl - (CONV_WIDTH - 1):, :]


def _lru_prompt(zx, zg, prm, j, batch, seq):
    tl = LRU_TILE
    nt = seq // tl
    row = pl.BlockSpec((tl, LRU_WIDTH), lambda b, t: (b * nt + t, 0))
    return pl.pallas_call(
        _lru_prompt_kernel,
        grid=(batch, nt),
        in_specs=[
            row, row,
            _layer_spec((CONV_WIDTH, LRU_WIDTH), j),
            _layer_spec((1, LRU_WIDTH), j),
            _layer_spec((LRU_WIDTH, 2 * LRU_WIDTH), j),
            _layer_spec((1, LRU_WIDTH), j),
            _layer_spec((1, LRU_WIDTH), j),
            _layer_spec((1, LRU_WIDTH), j),
        ],
        out_specs=(
            row,
            pl.BlockSpec((None, 1, LRU_WIDTH), lambda b, t: (b, 0, 0)),
            pl.BlockSpec((None, CONV_WIDTH - 1, LRU_WIDTH), lambda b, t: (b, 0, 0)),
        ),
        out_shape=(
            jax.ShapeDtypeStruct((batch * seq, LRU_WIDTH), BF16),
            jax.ShapeDtypeStruct((batch, 1, LRU_WIDTH), F32),
            jax.ShapeDtypeStruct((batch, CONV_WIDTH - 1, LRU_WIDTH), F32),
        ),
        scratch_shapes=[pltpu.VMEM((SUBLANES + tl, LRU_WIDTH), F32),
                        pltpu.VMEM((1, LRU_WIDTH), F32)],
        compiler_params=pltpu.CompilerParams(dimension_semantics=("parallel", "arbitrary"),
                                             vmem_limit_bytes=VMEM_LIMIT),
        name="lru_prompt",
    )(zx, zg, prm["conv_w"], prm["conv_b"], prm["w_gate_lru"], prm["b_rg"], prm["b_ig"],
      prm["lru_lambda"])


def _lru_sample_kernel(zx_ref, zg_ref, buf_ref, h0_ref, cw_ref, cb_ref, wgate_ref, brg_ref,
                       big_ref, lam_ref, y_ref, hlast_ref):
    nt = zx_ref.shape[0]
    nb = zx_ref.shape[1]
    xpad = [buf_ref[k] for k in range(CONV_WIDTH - 1)] + [zx_ref[k] for k in range(nt)]
    xcs = []
    for t in range(nt):
        xc = cb_ref[...] + xpad[t] * cw_ref[0:1, :]
        for k in range(1, CONV_WIDTH):
            xc = xc + xpad[t + k] * cw_ref[k:k + 1, :]
        xcs.append(xc)
    a, b = _lru_gates(jnp.concatenate(xcs, axis=0), wgate_ref, brg_ref, big_ref, lam_ref)
    h = h0_ref[...]
    for t in range(nt):
        h = a[t * nb:(t + 1) * nb] * h + b[t * nb:(t + 1) * nb]
        y_ref[t] = (h * jax.nn.gelu(zg_ref[t])).astype(y_ref.dtype)
    hlast_ref[...] = h


def _lru_sample(zx_t, zg_t, buf_t, h0, prm, j):
    nt, nb, _ = zx_t.shape
    return pl.pallas_call(
        _lru_sample_kernel,
        grid=(1,),
        in_specs=[
            _const_spec((nt, nb, LRU_WIDTH)), _const_spec((nt, nb, LRU_WIDTH)),
            _const_spec((CONV_WIDTH - 1, nb, LRU_WIDTH)), _const_spec((nb, LRU_WIDTH)),
            _layer_spec((CONV_WIDTH, LRU_WIDTH), j),
            _layer_spec((1, LRU_WIDTH), j),
            _layer_spec((LRU_WIDTH, 2 * LRU_WIDTH), j),
            _layer_spec((1, LRU_WIDTH), j),
            _layer_spec((1, LRU_WIDTH), j),
            _layer_spec((1, LRU_WIDTH), j),
        ],
        out_specs=(
            pl.BlockSpec((nt, nb, LRU_WIDTH), lambda i: (0, 0, 0)),
            pl.BlockSpec((nb, LRU_WIDTH), lambda i: (0, 0)),
        ),
        out_shape=(
            jax.ShapeDtypeStruct((nt, nb, LRU_WIDTH), BF16),
            jax.ShapeDtypeStruct((nb, LRU_WIDTH), F32),
        ),
        compiler_params=pltpu.CompilerParams(dimension_semantics=("arbitrary",),
                                             vmem_limit_bytes=VMEM_LIMIT),
        name="lru_sample",
    )(zx_t, zg_t, buf_t, h0, prm["conv_w"], prm["conv_b"], prm["w_gate_lru"], prm["b_rg"],
      prm["b_ig"], prm["lru_lambda"])


def _attn_prompt_kernel(q_ref, k_ref, v_ref, o_ref, m_sc, acc_sc):
    i = pl.program_id(1)
    tq = q_ref.shape[1]
    m_sc[...] = jnp.full(m_sc.shape, -jnp.inf, F32)
    acc_sc[...] = jnp.zeros(acc_sc.shape, F32)

    def key_block(start, width, masked):
        keys = pl.ds(start, width)
        if masked:
            q_pos = i * tq + lax.broadcasted_iota(jnp.int32, (tq, width), 0)
            causal = start + lax.broadcasted_iota(jnp.int32, (tq, width), 1) <= q_pos
        scores = [_dot_nt(q_ref[hh], k_ref[hh, keys, :]) for hh in range(MLA_HEADS)]
        for hh in range(MLA_HEADS):
            s = scores[hh]
            if masked:
                s = jnp.where(causal, s, -jnp.inf)
            m_prev = m_sc[hh]
            m_new = jnp.maximum(m_prev, jnp.max(s, axis=-1, keepdims=True))
            alpha = jnp.exp2(m_prev - m_new)
            p = jnp.exp2(s - jnp.concatenate([m_new] * (width // LANES), axis=1))
            acc_sc[hh] = alpha * acc_sc[hh] + _dot(p.astype(BF16), v_ref[hh, keys, :])
            m_sc[hh] = m_new

    def full_block(jb, carry):
        key_block(pl.multiple_of(jb * tq, tq), tq, False)
        return carry

    lax.fori_loop(0, i, full_block, 0)
    key_block(pl.multiple_of(i * tq, tq), tq, True)

    outs = []
    for hh in range(MLA_HEADS):
        acc = acc_sc[hh]
        outs.append(acc[:, :V_DIM] / acc[:, V_DIM:V_DIM + 1])
    o_ref[...] = jnp.concatenate(outs, axis=-1).astype(o_ref.dtype)


def _attn_prompt(q, k, v, batch, seq):
    tq = ATTN_TILE
    nq = seq // tq
    return pl.pallas_call(
        _attn_prompt_kernel,
        grid=(batch, nq),
        in_specs=[
            pl.BlockSpec((MLA_HEADS, tq, LANES), lambda b, i: (0, b * nq + i, 0)),
            pl.BlockSpec((MLA_HEADS, seq, LANES), lambda b, i: (0, b, 0)),
            pl.BlockSpec((MLA_HEADS, seq, LANES), lambda b, i: (0, b, 0)),
        ],
        out_specs=pl.BlockSpec((tq, MLA_HEADS * V_DIM), lambda b, i: (b * nq + i, 0)),
        out_shape=jax.ShapeDtypeStruct((batch * seq, MLA_HEADS * V_DIM), BF16),
        scratch_shapes=[pltpu.VMEM((MLA_HEADS, tq, LANES), F32),
                        pltpu.VMEM((MLA_HEADS, tq, LANES), F32)],
        compiler_params=pltpu.CompilerParams(dimension_semantics=("parallel", "arbitrary"),
                                             vmem_limit_bytes=VMEM_LIMIT),
        name="attn_prompt",
    )(q, k, v)


def _attn_sample_kernel(pt_ref, q_ref, knew_ref, ckv_hbm, kpe_hbm, o_ref, cbuf, pbuf, sems, *,
                        layer, n_pages):
    b = pl.program_id(0)
    nb = pl.num_programs(0)
    slot = b % 2

    def page_copies(bb, sl):
        copies = []
        for pg in range(n_pages):
            page = pt_ref[bb * n_pages + pg]
            rows = pl.ds(pg * PAGE_SIZE, PAGE_SIZE)
            copies.append(pltpu.make_async_copy(ckv_hbm.at[layer, page], cbuf.at[sl, rows],
                                                sems.at[0, sl]))
            copies.append(pltpu.make_async_copy(kpe_hbm.at[layer, page], pbuf.at[sl, :, rows],
                                                sems.at[1, sl]))
        return copies

    @pl.when(b == 0)
    def _():
        for cp in page_copies(b, slot):
            cp.start()

    @pl.when(b + 1 < nb)
    def _():
        for cp in page_copies(b + 1, 1 - slot):
            cp.start()

    for cp in page_copies(b, slot):
        cp.wait()

    q = q_ref[...]
    q_lat = q[:, :KV_LORA]
    o_pe = KV_LORA + QK_NOPE
    q_pe = q[:, o_pe:o_pe + QK_ROPE]
    knew = knew_ref[...]
    s_new = _dot_nt(q, knew) * ATTN_SCALE
    t_row = lax.broadcasted_iota(jnp.int32, s_new.shape, 0) // MLA_HEADS
    t_key = lax.broadcasted_iota(jnp.int32, s_new.shape, 1)
    s_new = jnp.where(t_key <= t_row, s_new, -jnp.inf)
    span = cbuf.shape[1] // SAMPLE_KEY_SPANS
    spans = [slice(c * span, (c + 1) * span) for c in range(SAMPLE_KEY_SPANS)]
    kcs = [cbuf[slot, sp, :].astype(BF16) for sp in spans]
    s_lat = [_dot_nt(q_lat, kc) for kc in kcs]
    s_pe = [_dot(q_pe, pbuf[slot, :, sp].astype(BF16)) for sp in spans]
    scores = [(a + b) * ATTN_SCALE for a, b in zip(s_lat, s_pe)]
    m = jnp.max(s_new, axis=-1, keepdims=True)
    for s in scores:
        m = jnp.maximum(m, jnp.max(s, axis=-1, keepdims=True))
    p_new = jnp.exp(s_new - m)
    denom = jnp.sum(p_new, axis=-1, keepdims=True)
    outs = [_dot(p_new.astype(BF16), knew[:, :KV_LORA])]
    for s, kc in zip(scores, kcs):
        p = jnp.exp(s - m)
        denom = denom + jnp.sum(p, axis=-1, keepdims=True)
        outs.append(_dot(p.astype(BF16), kc))
    while len(outs) > 1:
        outs = [outs[k] + outs[k + 1] if k + 1 < len(outs) else outs[k]
                for k in range(0, len(outs), 2)]
    o_ref[...] = (outs[0] / denom).astype(o_ref.dtype)


def _attn_sample(q_s, knew, cache_ckv, cache_kpe_t, page_table, j):
    nb, rows, _ = q_s.shape
    n_pages = page_table.shape[1]
    past = n_pages * PAGE_SIZE
    kernel = functools.partial(_attn_sample_kernel, layer=j, n_pages=n_pages)
    grid_spec = pltpu.PrefetchScalarGridSpec(
        num_scalar_prefetch=1,
        grid=(nb,),
        in_specs=[
            pl.BlockSpec((None, rows, QK_CAT), lambda b, pt: (b, 0, 0)),
            pl.BlockSpec((None, knew.shape[1], QK_CAT), lambda b, pt: (b, 0, 0)),
            pl.BlockSpec(memory_space=pl.ANY),
            pl.BlockSpec(memory_space=pl.ANY),
        ],
        out_specs=pl.BlockSpec((None, rows, KV_LORA), lambda b, pt: (b, 0, 0)),
        scratch_shapes=[
            pltpu.VMEM((2, past, KV_LORA), F32),
            pltpu.VMEM((2, QK_ROPE, past), F32),
            pltpu.SemaphoreType.DMA((2, 2)),
        ],
    )
    return pl.pallas_call(
        kernel,
        grid_spec=grid_spec,
        out_shape=jax.ShapeDtypeStruct((nb, rows, KV_LORA), BF16),
        compiler_params=pltpu.CompilerParams(dimension_semantics=("arbitrary",),
                                             vmem_limit_bytes=VMEM_LIMIT),
        name="attn_sample",
    )(page_table.reshape(-1), q_s, knew, cache_ckv, cache_kpe_t)


def _uv_kernel(o_ref, wuv_ref, a_ref):
    outs = [_dot(o_ref[:, hh * KV_LORA:(hh + 1) * KV_LORA], wuv_ref[hh])
            for hh in range(MLA_HEADS)]
    a_ref[...] = jnp.concatenate(outs, axis=-1).astype(a_ref.dtype)


def _uv_project(o_lat, prm, j):
    n = o_lat.shape[0]
    return pl.pallas_call(
        _uv_kernel,
        grid=(1,),
        in_specs=[_const_spec((n, MLA_HEADS * KV_LORA)),
                  _layer_spec((MLA_HEADS, KV_LORA, V_DIM), j)],
        out_specs=pl.BlockSpec((n, MLA_HEADS * V_DIM), lambda i: (0, 0)),
        out_shape=jax.ShapeDtypeStruct((n, MLA_HEADS * V_DIM), BF16),
        compiler_params=pltpu.CompilerParams(dimension_semantics=("arbitrary",),
                                             vmem_limit_bytes=VMEM_LIMIT),
        name="uv_project",
    )(o_lat, prm["w_uv"])


def _ffn_and_gate(h1, p_ref, gffn_ref, wg_ref, wu_ref, wd_ref, gpe_ref, wpg_ref, wpe_ref):
    hn = _rms(h1, gffn_ref[...]).astype(BF16)
    act = (jax.nn.silu(_dot(hn, wg_ref[...])) * _dot(hn, wu_ref[...])).astype(BF16)
    h2 = h1 + _dot(act, wd_ref[...])
    gate = jax.nn.sigmoid(_dot(_rms(h2, gpe_ref[...]).astype(BF16), wpg_ref[...]))
    return h2 + _dot(p_ref[...].astype(BF16), wpe_ref[...]) * gate


def _post_ab_kernel(h_ref, attn_ref, lru_ref, p_ref, wout_ref, gffn_ref, wg_ref, wu_ref, wd_ref,
                    gpe_ref, wpg_ref, wpe_ref, out_ref):
    n_attn = MLA_HEADS * V_DIM
    mix = _dot(attn_ref[...], wout_ref[:n_attn, :]) + _dot(lru_ref[...], wout_ref[n_attn:, :])
    h1 = h_ref[...] + mix
    out_ref[...] = _ffn_and_gate(h1, p_ref, gffn_ref, wg_ref, wu_ref, wd_ref, gpe_ref, wpg_ref,
                                 wpe_ref)


def _post_c_kernel(h_ref, p_ref, gmix_ref, winc_ref, lng_ref, lnb_ref, ws_ref, bs_ref, woutc_ref,
                   gffn_ref, wg_ref, wu_ref, wd_ref, gpe_ref, wpg_ref, wpe_ref, gfin_ref,
                   out_ref, *rest, block_len, final):
    sp_sc = rest[-1]
    h = h_ref[...]
    tm = h.shape[0]
    z = jax.nn.gelu(_dot(_rms(h, gmix_ref[...]).astype(BF16), winc_ref[...]))
    u = z[:, :GMLP_WIDTH]
    v = z[:, GMLP_WIDTH:]
    vc = v - jnp.mean(v, axis=-1, keepdims=True)
    v = vc * lax.rsqrt(jnp.mean(vc * vc, axis=-1, keepdims=True) + EPS) * lng_ref[...] + lnb_ref[...]
    if len(rest) == 2:
        rest[0][...] = v
    vb = v.astype(BF16)
    t_idx = lax.broadcasted_iota(jnp.int32, (CHUNK, CHUNK), 0)
    s_idx = lax.broadcasted_iota(jnp.int32, (CHUNK, CHUNK), 1)
    mask = (s_idx <= t_idx) & ((s_idx // block_len) == (t_idx // block_len))
    for gg in range(GMLP_GROUPS):
        wm = jnp.where(mask, ws_ref[gg], 0.0).astype(BF16)
        cols = slice(gg * GMLP_GROUP_DIM, (gg + 1) * GMLP_GROUP_DIM)
        for cc in range(tm // CHUNK):
            rows = slice(cc * CHUNK, (cc + 1) * CHUNK)
            sp_sc[rows, cols] = _dot(wm, vb[rows, cols]) + bs_ref[:, cols]
    mix = _dot((u * sp_sc[...]).astype(BF16), woutc_ref[...])
    h3 = _ffn_and_gate(h + mix, p_ref, gffn_ref, wg_ref, wu_ref, wd_ref, gpe_ref, wpg_ref, wpe_ref)
    out_ref[...] = _rms(h3, gfin_ref[...]) if final else h3


def _ffn_specs(i):
    return [
        _layer_spec((1, D_MODEL), i),
        _layer_spec((D_MODEL, FFN_HIDDEN), i),
        _layer_spec((D_MODEL, FFN_HIDDEN), i),
        _layer_spec((FFN_HIDDEN, D_MODEL), i),
        _layer_spec((1, D_MODEL), i),
        _layer_spec((D_MODEL, D_MODEL), i),
        _layer_spec((PLE_DIM, D_MODEL), i),
    ]


def _ffn_args(prm):
    return (prm["g_ffn"], prm["w_gate"], prm["w_up"], prm["w_down"], prm["g_pe"], prm["w_pg"],
            prm["w_pe"])


def _post_ab(h, attn, lru_y, p, prm, i):
    n = h.shape[0]
    tm = POST_AB_TILE
    j = i // 2
    row = lambda w: pl.BlockSpec((tm, w), lambda r: (r, 0))
    return pl.pallas_call(
        _post_ab_kernel,
        grid=(n // tm,),
        in_specs=[
            row(D_MODEL), row(MLA_HEADS * V_DIM), row(LRU_WIDTH),
            pl.BlockSpec((None, tm, PLE_DIM), lambda r: (i, r, 0)),
            _layer_spec((MLA_HEADS * V_DIM + LRU_WIDTH, D_MODEL), j),
        ] + _ffn_specs(i),
        out_specs=row(D_MODEL),
        out_shape=jax.ShapeDtypeStruct((n, D_MODEL), F32),
        compiler_params=pltpu.CompilerParams(dimension_semantics=("parallel",),
                                             vmem_limit_bytes=VMEM_LIMIT),
        name="post_ab",
    )(h, attn, lru_y, p, prm["w_out_ab"], *_ffn_args(prm))


def _post_c(h, p, prm, i, ws_tiled, bs_full, block_len, emit_v):
    n = h.shape[0]
    tm = POST_C_TILE
    j = i // 2
    final = i == DEPTH - 1
    row = lambda w: pl.BlockSpec((tm, w), lambda r: (r, 0))
    kernel = functools.partial(_post_c_kernel, block_len=block_len, final=final)
    out_specs = [row(D_MODEL)] + ([row(GMLP_WIDTH)] if emit_v else [])
    out_shape = [jax.ShapeDtypeStruct((n, D_MODEL), F32)]
    if emit_v:
        out_shape.append(jax.ShapeDtypeStruct((n, GMLP_WIDTH), F32))
    outs = pl.pallas_call(
        kernel,
        grid=(n // tm,),
        in_specs=[
            row(D_MODEL),
            pl.BlockSpec((None, tm, PLE_DIM), lambda r: (i, r, 0)),
            _layer_spec((1, D_MODEL), i),
            _layer_spec((D_MODEL, 2 * GMLP_WIDTH), j),
            _layer_spec((1, GMLP_WIDTH), j),
            _layer_spec((1, GMLP_WIDTH), j),
            _layer_spec((GMLP_GROUPS, CHUNK, CHUNK), j),
            _layer_spec((CHUNK, GMLP_WIDTH), j),
            _layer_spec((GMLP_WIDTH, D_MODEL), j),
        ] + _ffn_specs(i) + [_const_spec((1, D_MODEL))],
        out_specs=out_specs,
        out_shape=out_shape,
        scratch_shapes=[pltpu.VMEM((tm, GMLP_WIDTH), F32)],
        compiler_params=pltpu.CompilerParams(dimension_semantics=("parallel",),
                                             vmem_limit_bytes=VMEM_LIMIT),
        name="post_c",
    )(h, p, prm["g_mix"], prm["w_in_c"], prm["ln_g_c"], prm["ln_b_c"], ws_tiled, bs_full,
      prm["w_out_c"], *_ffn_args(prm), prm["g_final"])
    return outs[0], (outs[1] if emit_v else None)


def _rope_tables(pos, reps=1):
    inv = jnp.exp(-math.log(ROPE_THETA) * jnp.arange(ROPE_HALF, dtype=F32) / ROPE_HALF)
    ang = pos.astype(F32)[:, None] * inv[None, :]
    cos, sin = jnp.cos(ang), jnp.sin(ang)
    t = pos.shape[0]
    head = jnp.zeros((t, QK_NOPE), F32)
    tail = jnp.zeros((t, LANES - QK_NOPE - QK_ROPE), F32)
    csk = jnp.concatenate([head, cos, cos, tail], axis=1)
    snk = jnp.concatenate([head, -sin, sin, tail], axis=1)
    log2_scale = ATTN_SCALE * math.log2(math.e)
    csq = jnp.concatenate([head + 1.0, cos, cos, tail], axis=1) * log2_scale
    tables = {"csk": csk, "snk": snk, "csq": csq, "snq": snk * log2_scale}
    return {name: jnp.tile(tab, (reps, 1)) for name, tab in tables.items()}


def _prepare_params(g_mix, g_ffn, g_pe, g_final, w_in_ab, g_qnorm, g_kvnorm, w_uq, w_uk, w_uv,
                    conv_w, conv_b, w_rg, b_rg, w_ig, b_ig, lru_lambda, w_out_ab, w_in_c, ln_g_c,
                    ln_b_c, w_s, b_s, w_out_c, w_gate, w_up, w_down, w_pe, w_pg):
    n_ab = w_in_ab.shape[0]
    o1 = Q_LORA + KV_LORA
    o2 = o1 + QK_ROPE
    pad_tail = LANES - QK_NOPE - QK_ROPE - ROPE_HALF
    w_in = jnp.concatenate(
        [w_in_ab[:, :, :o1], w_in_ab[:, :, o2:], jnp.zeros((n_ab, D_MODEL, QK_NOPE), F32),
         w_in_ab[:, :, o1:o2], w_in_ab[:, :, o1:o1 + ROPE_HALF],
         jnp.zeros((n_ab, D_MODEL, pad_tail), F32)], axis=2)
    uq = w_uq.reshape(n_ab, Q_LORA, MLA_HEADS, QK_NOPE + QK_ROPE)
    uq = jnp.concatenate([uq, uq[..., QK_NOPE:QK_NOPE + ROPE_HALF],
                          jnp.zeros(uq.shape[:3] + (pad_tail,), F32)], axis=3)
    uq = uq.reshape(n_ab, Q_LORA, MLA_HEADS * LANES)
    uk_t = jnp.pad(w_uk.transpose(0, 3, 1, 2), ((0, 0), (0, 0), (0, 0), (0, LANES - QK_NOPE)))
    uk_t = uk_t.reshape(n_ab, KV_LORA, MLA_HEADS * LANES)
    uv_cat = jnp.pad(w_uv.transpose(0, 2, 1, 3), ((0, 0), (0, 0), (0, 0), (0, LANES - V_DIM)))
    uv_cat = uv_cat.reshape(n_ab, KV_LORA, MLA_HEADS * LANES)

    def block_diag(w):
        eye = jnp.eye(LRU_HEADS, dtype=F32)
        return jnp.einsum("jhab,hg->jhagb", w, eye).reshape(n_ab, LRU_WIDTH, LRU_WIDTH)

    vec = lambda a: a[:, None, :]
    return {
        "g_mix": vec(g_mix), "g_ffn": vec(g_ffn), "g_pe": vec(g_pe), "g_final": g_final[None, :],
        "w_in_ab": w_in.astype(BF16), "g_qnorm": vec(g_qnorm), "g_kvnorm": vec(g_kvnorm),
        "w_uq": uq.astype(BF16),
        "w_uk": w_uk.astype(BF16), "w_uv": w_uv.astype(BF16),
        "w_uk_t": uk_t.astype(BF16), "w_uv_cat": uv_cat.astype(BF16),
        "conv_w": conv_w, "conv_b": vec(conv_b),
        "w_gate_lru": jnp.concatenate([block_diag(w_rg), block_diag(w_ig)], axis=2).astype(BF16),
        "b_rg": vec(b_rg), "b_ig": vec(b_ig), "lru_lambda": vec(lru_lambda),
        "w_out_ab": w_out_ab.astype(BF16),
        "w_in_c": w_in_c.astype(BF16), "ln_g_c": vec(ln_g_c), "ln_b_c": vec(ln_b_c),
        "w_s": w_s, "b_s": b_s, "w_out_c": w_out_c.astype(BF16),
        "w_gate": w_gate.astype(BF16), "w_up": w_up.astype(BF16), "w_down": w_down.astype(BF16),
        "w_pe": w_pe.astype(BF16), "w_pg": w_pg.astype(BF16),
    }


def _gmlp_spatial(prm, block_len):
    reps = CHUNK // block_len
    ws = jnp.tile(prm["w_s"][:, :, :block_len, :block_len], (1, 1, reps, reps))
    bs = jnp.tile(prm["b_s"][:, :, :block_len], (1, 1, reps))
    bs = jnp.repeat(jnp.swapaxes(bs, 1, 2), GMLP_GROUP_DIM, axis=2)
    return ws, bs


def _trunk(x, p, tables, n_pos_tiles, prm, past, batch, seq):
    n = batch * seq
    h = x.reshape(n, D_MODEL)
    p = p.reshape(DEPTH, n, PLE_DIM)
    block_len = min(seq, CHUNK)
    ws_tiled, bs_full = _gmlp_spatial(prm, block_len)
    ckv, kpe, lru, conv, vrows = [], [], [], [], []
    for i in range(DEPTH):
        j = i // 2
        if i % 2 == 0:
            if past is None:
                q, k, v, c_kv, k_pe, zx, zg = _in_even(h, tables, prm, j, n_pos_tiles, False)
                attn = _attn_prompt(q, k, v, batch, seq)
                lru_y, h_last, buf = _lru_prompt(zx, zg, prm, j, batch, seq)
                h_last = h_last.reshape(batch, LRU_WIDTH)
            else:
                qcat, kcat, c_kv, k_pe, zx, zg = _in_even(h, tables, prm, j, n_pos_tiles, True)
                q_s = qcat.reshape(MLA_HEADS, batch, seq, QK_CAT).transpose(1, 2, 0, 3)
                q_s = q_s.reshape(batch, seq * MLA_HEADS, QK_CAT)
                knew = jnp.pad(kcat.reshape(batch, seq, QK_CAT),
                               ((0, 0), (0, 2 * SUBLANES - seq), (0, 0)))
                o_lat = _attn_sample(q_s, knew, past["cache_ckv"], past["cache_kpe_t"],
                                     past["page_table"], j)
                attn = _uv_project(o_lat.reshape(n, MLA_HEADS * KV_LORA), prm, j)
                to_t = lambda a: a.reshape(batch, seq, LRU_WIDTH).transpose(1, 0, 2)
                y_t, h_last = _lru_sample(to_t(zx), to_t(zg),
                                          past["state_conv"][j].transpose(1, 0, 2),
                                          past["state_lru"][j], prm, j)
                lru_y = y_t.transpose(1, 0, 2).reshape(n, LRU_WIDTH)
                buf = jnp.concatenate([past["state_conv"][j], zx.reshape(batch, seq, LRU_WIDTH)],
                                      axis=1)[:, seq:]
            h = _post_ab(h, attn, lru_y, p, prm, i)
            ckv.append(c_kv.reshape(batch, seq, KV_LORA))
            kpe.append(k_pe.reshape(batch, seq, QK_ROPE))
            lru.append(h_last)
            conv.append(buf)
        else:
            h, v = _post_c(h, p, prm, i, ws_tiled, bs_full, block_len, past is not None)
            if v is not None:
                vrows.append(v.reshape(batch, seq, GMLP_WIDTH))
    return h.reshape(batch, seq, D_MODEL), ckv, kpe, lru, conv, vrows


def kernel(x_prompt, x_sample, cache_ckv, cache_kpe, state_lru, state_conv, page_table, p_prompt, p_sample, g_mix, g_ffn, g_pe, g_final, w_in_ab, g_qnorm, g_kvnorm, w_uq, w_uk, w_uv, conv_w, conv_b, w_rg, b_rg, w_ig, b_ig, lru_lambda, w_out_ab, w_in_c, ln_g_c, ln_b_c, w_s, b_s, w_out_c, w_gate, w_up, w_down, w_pe, w_pg):
    prm = _prepare_params(g_mix, g_ffn, g_pe, g_final, w_in_ab, g_qnorm, g_kvnorm, w_uq, w_uk,
                          w_uv, conv_w, conv_b, w_rg, b_rg, w_ig, b_ig, lru_lambda, w_out_ab,
                          w_in_c, ln_g_c, ln_b_c, w_s, b_s, w_out_c, w_gate, w_up, w_down, w_pe,
                          w_pg)
    batch, seq, _ = x_prompt.shape
    dec_batch, dec_seq, _ = x_sample.shape
    past_len = page_table.shape[1] * PAGE_SIZE

    tables_p = _rope_tables(jnp.arange(seq, dtype=jnp.int32))
    y_prompt, ckv_p, kpe_p, lru_p, conv_p, _ = _trunk(
        x_prompt, p_prompt, tables_p, seq // IN_TILE, prm, None, batch, seq)

    tables_s = _rope_tables(past_len + jnp.arange(dec_seq, dtype=jnp.int32), IN_TILE // dec_seq)
    past = {"cache_ckv": cache_ckv, "cache_kpe_t": jnp.swapaxes(cache_kpe, 2, 3), "state_lru": state_lru,
            "state_conv": state_conv, "page_table": page_table}
    y_sample, ckv_s, kpe_s, lru_s, conv_s, v_s = _trunk(
        x_sample, p_sample, tables_s, 1, prm, past, dec_batch, dec_seq)

    return (y_prompt, y_sample,
            jnp.stack(ckv_p), jnp.stack(kpe_p), jnp.stack(lru_p), jnp.stack(conv_p),
            jnp.stack(ckv_s), jnp.stack(kpe_s), jnp.stack(lru_s), jnp.stack(conv_s),
            jnp.stack(v_s))
```

```python
import functools
import math

import jax
import jax.numpy as jnp
from jax import lax
from jax.experimental import pallas as pl
from jax.experimental.pallas import tpu as pltpu

F32 = jnp.float32
BF16 = jnp.bfloat16

D_MODEL = 1024
DEPTH = 4
PAGE_SIZE = 128
MLA_HEADS = 8
Q_LORA = 384
KV_LORA = 256
QK_NOPE = 64
QK_ROPE = 32
ROPE_HALF = QK_ROPE // 2
V_DIM = 64
ROPE_THETA = 10000.0
ATTN_SCALE = 1.0 / math.sqrt(QK_NOPE + QK_ROPE)
LRU_WIDTH = 512
LRU_HEADS = 8
LRU_HEAD_DIM = LRU_WIDTH // LRU_HEADS
CONV_WIDTH = 4
LRU_C = 8.0
CHUNK = 128
GMLP_WIDTH = 1024
GMLP_GROUPS = 8
GMLP_GROUP_DIM = GMLP_WIDTH // GMLP_GROUPS
FFN_HIDDEN = -(-8 * D_MODEL // (3 * 256)) * 256
PLE_DIM = 256
EPS = 1e-6

LANES = 128
SUBLANES = 8
QK_CAT = KV_LORA + LANES
IN_AB_PAD = Q_LORA + KV_LORA + 2 * LRU_WIDTH + LANES
VMEM_LIMIT = 60 * 1024 * 1024

IN_TILE = 512
POST_AB_TILE = 512
POST_C_TILE = 512
ATTN_TILE = 512
LRU_TILE = 1024
SAMPLE_KEY_SPANS = 4


def _const_spec(shape):
    nd = len(shape)
    return pl.BlockSpec(shape, lambda *_: (0,) * nd, pipeline_mode=pl.Buffered(1))


def _layer_spec(shape, layer):
    nd = len(shape)
    return pl.BlockSpec((None,) + tuple(shape), lambda *_: (layer,) + (0,) * nd,
                        pipeline_mode=pl.Buffered(1))


def _rms(x, g):
    return x * lax.rsqrt(jnp.mean(x * x, axis=-1, keepdims=True) + EPS) * g


def _dot(a, b):
    return jnp.dot(a, b, preferred_element_type=F32)


def _dot_nt(a, b):
    return lax.dot_general(a, b, (((1,), (1,)), ((), ())), preferred_element_type=F32)


def _rope_block(blk, cs, sn):
    partner = pltpu.roll(blk, LANES - ROPE_HALF, 1)
    return blk * cs + partner * sn


def _in_even_common(h_ref, csk_ref, snk_ref, gmix_ref, win_ref, gq_ref, wuq_ref, gkv_ref,
                    ckv_ref, kpe_ref, zx_ref, zg_ref):
    xn = _rms(h_ref[...], gmix_ref[...]).astype(BF16)
    z = _dot(xn, win_ref[...])
    o_kv = Q_LORA
    o_x = o_kv + KV_LORA
    o_g = o_x + LRU_WIDTH
    o_pe = o_g + LRU_WIDTH
    zx_ref[...] = z[:, o_x:o_g]
    zg_ref[...] = z[:, o_g:o_pe]
    ckv = _rms(z[:, o_kv:o_x], gkv_ref[...])
    kpe_blk = _rope_block(z[:, o_pe:o_pe + LANES], csk_ref[...], snk_ref[...])
    ckv_ref[...] = ckv
    kpe_ref[...] = kpe_blk[:, QK_NOPE:QK_NOPE + QK_ROPE]
    qn = _rms(z[:, :Q_LORA], gq_ref[...]).astype(BF16)
    q = _dot(qn, wuq_ref[...])
    return ckv, kpe_blk, q


def _in_even_sample_kernel(h_ref, csk_ref, snk_ref, gmix_ref, win_ref, gq_ref, wuq_ref, gkv_ref,
                           wuk_ref, qcat_ref, kcat_ref, ckv_ref, kpe_ref, zx_ref, zg_ref):
    ckv, kpe_blk, q = _in_even_common(h_ref, csk_ref, snk_ref, gmix_ref, win_ref, gq_ref, wuq_ref,
                                      gkv_ref, ckv_ref, kpe_ref, zx_ref, zg_ref)
    kcat_ref[:, :KV_LORA] = ckv.astype(BF16)
    kcat_ref[:, KV_LORA:] = kpe_blk.astype(BF16)
    for hh in range(MLA_HEADS):
        blk = q[:, hh * LANES:(hh + 1) * LANES]
        qcat_ref[hh, :, :KV_LORA] = _dot(blk[:, :QK_NOPE].astype(BF16), wuk_ref[hh]).astype(BF16)
        qcat_ref[hh, :, KV_LORA:] = _rope_block(blk, csk_ref[...], snk_ref[...]).astype(BF16)


def _in_even_prompt_kernel(h_ref, csk_ref, snk_ref, gmix_ref, win_ref, gq_ref, wuq_ref, gkv_ref,
                           csq_ref, snq_ref, wukt_ref, wuvc_ref,
                           q_ref, k_ref, v_ref, ckv_ref, kpe_ref, zx_ref, zg_ref):
    ckv, kpe_blk, q = _in_even_common(h_ref, csk_ref, snk_ref, gmix_ref, win_ref, gq_ref, wuq_ref,
                                      gkv_ref, ckv_ref, kpe_ref, zx_ref, zg_ref)
    cb = ckv.astype(BF16)
    k_nope = _dot(cb, wukt_ref[...])
    v = _dot(cb, wuvc_ref[...])
    ones_lane = (lax.broadcasted_iota(jnp.int32, (1, LANES), 1) == V_DIM).astype(F32)
    for hh in range(MLA_HEADS):
        blk = slice(hh * LANES, (hh + 1) * LANES)
        q_ref[hh] = _rope_block(q[:, blk], csq_ref[...], snq_ref[...]).astype(BF16)
        k_ref[hh] = (k_nope[:, blk] + kpe_blk).astype(BF16)
        v_ref[hh] = (v[:, blk] + ones_lane).astype(BF16)


def _in_even(h, tables, prm, j, n_pos_tiles, absorbed):
    n = h.shape[0]
    tm = IN_TILE
    row = lambda w: pl.BlockSpec((tm, w), lambda r: (r, 0))
    heads = lambda w: pl.BlockSpec((MLA_HEADS, tm, w), lambda r: (0, r, 0))
    pos = pl.BlockSpec((tm, LANES), lambda r: (r % n_pos_tiles, 0))
    in_specs = [
        row(D_MODEL), pos, pos,
        _layer_spec((1, D_MODEL), 2 * j),
        _layer_spec((D_MODEL, IN_AB_PAD), j),
        _layer_spec((1, Q_LORA), j),
        _layer_spec((Q_LORA, MLA_HEADS * LANES), j),
        _layer_spec((1, KV_LORA), j),
    ]
    args = [h, tables["csk"], tables["snk"], prm["g_mix"], prm["w_in_ab"], prm["g_qnorm"],
            prm["w_uq"], prm["g_kvnorm"]]
    tail_specs = (row(KV_LORA), row(QK_ROPE), row(LRU_WIDTH), row(LRU_WIDTH))
    tail_shapes = (
        jax.ShapeDtypeStruct((n, KV_LORA), F32),
        jax.ShapeDtypeStruct((n, QK_ROPE), F32),
        jax.ShapeDtypeStruct((n, LRU_WIDTH), F32),
        jax.ShapeDtypeStruct((n, LRU_WIDTH), F32),
    )
    if absorbed:
        body = _in_even_sample_kernel
        in_specs += [_layer_spec((MLA_HEADS, QK_NOPE, KV_LORA), j)]
        args += [prm["w_uk"]]
        out_specs = (heads(QK_CAT), row(QK_CAT)) + tail_specs
        out_shape = (jax.ShapeDtypeStruct((MLA_HEADS, n, QK_CAT), BF16),
                     jax.ShapeDtypeStruct((n, QK_CAT), BF16)) + tail_shapes
    else:
        body = _in_even_prompt_kernel
        in_specs += [pos, pos,
                     _layer_spec((KV_LORA, MLA_HEADS * LANES), j),
                     _layer_spec((KV_LORA, MLA_HEADS * LANES), j)]
        args += [tables["csq"], tables["snq"], prm["w_uk_t"], prm["w_uv_cat"]]
        out_specs = (heads(LANES), heads(LANES), heads(LANES)) + tail_specs
        out_shape = (jax.ShapeDtypeStruct((MLA_HEADS, n, LANES), BF16),) * 3 + tail_shapes
    return pl.pallas_call(
        body,
        grid=(n // tm,),
        in_specs=in_specs,
        out_specs=out_specs,
        out_shape=out_shape,
        compiler_params=pltpu.CompilerParams(dimension_semantics=("parallel",),
                                             vmem_limit_bytes=VMEM_LIMIT),
        name="in_even_absorbed" if absorbed else "in_even_heads",
    )(*args)


def _lru_gates(xc, wgate_ref, brg_ref, big_ref, lam_ref):
    g = _dot(xc.astype(BF16), wgate_ref[...])
    r = jax.nn.sigmoid(g[:, :LRU_WIDTH] + brg_ref[...])
    ig = jax.nn.sigmoid(g[:, LRU_WIDTH:] + big_ref[...])
    x = -lam_ref[...]
    softplus = jnp.maximum(x, 0.0) + jnp.log1p(jnp.exp(-jnp.abs(x)))
    log_a = -LRU_C * r * softplus
    a = jnp.exp(log_a)
    th = jnp.tanh(log_a)
    bx = jnp.sqrt(-2.0 * th / (1.0 - th)) * ig * xc
    return a, bx


def _shift_in_group(x, s, fill):
    g, _, w = x.shape
    return jnp.concatenate([jnp.full((g, s, w), fill, x.dtype), x[:, :-s, :]], axis=1)


def _lru_prompt_kernel(zx_ref, zg_ref, cw_ref, cb_ref, wgate_ref, brg_ref, big_ref, lam_ref,
                       y_ref, hlast_ref, conv_ref, xbuf, hprev):
    t = pl.program_id(1)
    tl = zx_ref.shape[0]
    pad = SUBLANES

    @pl.when(t == 0)
    def _():
        xbuf[:pad, :] = jnp.zeros((pad, LRU_WIDTH), F32)
        hprev[...] = jnp.zeros_like(hprev)

    x = zx_ref[...]
    xbuf[pad:, :] = x
    xc = cb_ref[...] + xbuf[pad - 3:pad - 3 + tl, :] * cw_ref[0:1, :]
    xc = xc + xbuf[pad - 2:pad - 2 + tl, :] * cw_ref[1:2, :]
    xc = xc + xbuf[pad - 1:pad - 1 + tl, :] * cw_ref[2:3, :]
    xc = xc + x * cw_ref[3:4, :]
    xbuf[:pad, :] = x[tl - pad:, :]

    a, b = _lru_gates(xc, wgate_ref, brg_ref, big_ref, lam_ref)
    groups = tl // SUBLANES
    a = a.reshape(groups, SUBLANES, LRU_WIDTH)
    b = b.reshape(groups, SUBLANES, LRU_WIDTH)
    s = 1
    while s < SUBLANES:
        b = a * _shift_in_group(b, s, 0.0) + b
        a = a * _shift_in_group(a, s, 1.0)
        s *= 2
    h = hprev[...]
    hs = []
    for g in range(groups):
        hs_g = a[g] * h + b[g]
        h = hs_g[SUBLANES - 1:, :]
        hs.append(hs_g)
    hs = jnp.concatenate(hs, axis=0)
    hprev[...] = h
    y_ref[...] = (hs * jax.nn.gelu(zg_ref[...])).astype(y_ref.dtype)

    @pl.when(t == pl.num_programs(1) - 1)
    def _():
        hlast_ref[...] = h
        conv_ref[...] = x[tl - (CONV_WIDTH - 1):, :]


def _lru_prompt(zx, zg, prm, j, batch, seq):
    tl = LRU_TILE
    nt = seq // tl
    row = pl.BlockSpec((tl, LRU_WIDTH), lambda b, t: (b * nt + t, 0))
    return pl.pallas_call(
        _lru_prompt_kernel,
        grid=(batch, nt),
        in_specs=[
            row, row,
            _layer_spec((CONV_WIDTH, LRU_WIDTH), j),
            _layer_spec((1, LRU_WIDTH), j),
            _layer_spec((LRU_WIDTH, 2 * LRU_WIDTH), j),
            _layer_spec((1, LRU_WIDTH), j),
            _layer_spec((1, LRU_WIDTH), j),
            _layer_spec((1, LRU_WIDTH), j),
        ],
        out_specs=(
            row,
            pl.BlockSpec((None, 1, LRU_WIDTH), lambda b, t: (b, 0, 0)),
            pl.BlockSpec((None, CONV_WIDTH - 1, LRU_WIDTH), lambda b, t: (b, 0, 0)),
        ),
        out_shape=(
            jax.ShapeDtypeStruct((batch * seq, LRU_WIDTH), BF16),
            jax.ShapeDtypeStruct((batch, 1, LRU_WIDTH), F32),
            jax.ShapeDtypeStruct((batch, CONV_WIDTH - 1, LRU_WIDTH), F32),
        ),
        scratch_shapes=[pltpu.VMEM((SUBLANES + tl, LRU_WIDTH), F32),
                        pltpu.VMEM((1, LRU_WIDTH), F32)],
        compiler_params=pltpu.CompilerParams(dimension_semantics=("parallel", "arbitrary"),
                                             vmem_limit_bytes=VMEM_LIMIT),
        name="lru_prompt",
    )(zx, zg, prm["conv_w"], prm["conv_b"], prm["w_gate_lru"], prm["b_rg"], prm["b_ig"],
      prm["lru_lambda"])


def _lru_sample_kernel(zx_ref, zg_ref, buf_ref, h0_ref, cw_ref, cb_ref, wgate_ref, brg_ref,
                       big_ref, lam_ref, y_ref, hlast_ref):
    nt = zx_ref.shape[0]
    nb = zx_ref.shape[1]
    xpad = [buf_ref[k] for k in range(CONV_WIDTH - 1)] + [zx_ref[k] for k in range(nt)]
    xcs = []
    for t in range(nt):
        xc = cb_ref[...] + xpad[t] * cw_ref[0:1, :]
        for k in range(1, CONV_WIDTH):
            xc = xc + xpad[t + k] * cw_ref[k:k + 1, :]
        xcs.append(xc)
    a, b = _lru_gates(jnp.concatenate(xcs, axis=0), wgate_ref, brg_ref, big_ref, lam_ref)
    h = h0_ref[...]
    for t in range(nt):
        h = a[t * nb:(t + 1) * nb] * h + b[t * nb:(t + 1) * nb]
        y_ref[t] = (h * jax.nn.gelu(zg_ref[t])).astype(y_ref.dtype)
    hlast_ref[...] = h


def _lru_sample(zx_t, zg_t, buf_t, h0, prm, j):
    nt, nb, _ = zx_t.shape
    return pl.pallas_call(
        _lru_sample_kernel,
        grid=(1,),
        in_specs=[
            _const_spec((nt, nb, LRU_WIDTH)), _const_spec((nt, nb, LRU_WIDTH)),
            _const_spec((CONV_WIDTH - 1, nb, LRU_WIDTH)), _const_spec((nb, LRU_WIDTH)),
            _layer_spec((CONV_WIDTH, LRU_WIDTH), j),
            _layer_spec((1, LRU_WIDTH), j),
            _layer_spec((LRU_WIDTH, 2 * LRU_WIDTH), j),
            _layer_spec((1, LRU_WIDTH), j),
            _layer_spec((1, LRU_WIDTH), j),
            _layer_spec((1, LRU_WIDTH), j),
        ],
        out_specs=(
            pl.BlockSpec((nt, nb, LRU_WIDTH), lambda i: (0, 0, 0)),
            pl.BlockSpec((nb, LRU_WIDTH), lambda i: (0, 0)),
        ),
        out_shape=(
            jax.ShapeDtypeStruct((nt, nb, LRU_WIDTH), BF16),
            jax.ShapeDtypeStruct((nb, LRU_WIDTH), F32),
        ),
        compiler_params=pltpu.CompilerParams(dimension_semantics=("arbitrary",),
                                             vmem_limit_bytes=VMEM_LIMIT),
        name="lru_sample",
    )(zx_t, zg_t, buf_t, h0, prm["conv_w"], prm["conv_b"], prm["w_gate_lru"], prm["b_rg"],
      prm["b_ig"], prm["lru_lambda"])


def _attn_prompt_kernel(q_ref, k_ref, v_ref, o_ref, m_sc, acc_sc):
    i = pl.program_id(1)
    tq = q_ref.shape[1]
    m_sc[...] = jnp.full(m_sc.shape, -jnp.inf, F32)
    acc_sc[...] = jnp.zeros(acc_sc.shape, F32)

    def key_block(start, width, masked):
        keys = pl.ds(start, width)
        if masked:
            q_pos = i * tq + lax.broadcasted_iota(jnp.int32, (tq, width), 0)
            causal = start + lax.broadcasted_iota(jnp.int32, (tq, width), 1) <= q_pos
        scores = [_dot_nt(q_ref[hh], k_ref[hh, keys, :]) for hh in range(MLA_HEADS)]
        for hh in range(MLA_HEADS):
            s = scores[hh]
            if masked:
                s = jnp.where(causal, s, -jnp.inf)
            m_prev = m_sc[hh]
            m_new = jnp.maximum(m_prev, jnp.max(s, axis=-1, keepdims=True))
            alpha = jnp.exp2(m_prev - m_new)
            p = jnp.exp2(s - jnp.concatenate([m_new] * (width // LANES), axis=1))
            acc_sc[hh] = alpha * acc_sc[hh] + _dot(p.astype(BF16), v_ref[hh, keys, :])
            m_sc[hh] = m_new

    def full_block(jb, carry):
        key_block(pl.multiple_of(jb * tq, tq), tq, False)
        return carry

    lax.fori_loop(0, i, full_block, 0)
    key_block(pl.multiple_of(i * tq, tq), tq, True)

    outs = []
    for hh in range(MLA_HEADS):
        acc = acc_sc[hh]
        outs.append(acc[:, :V_DIM] / acc[:, V_DIM:V_DIM + 1])
    o_ref[...] = jnp.concatenate(outs, axis=-1).astype(o_ref.dtype)


def _attn_prompt(q, k, v, batch, seq):
    tq = ATTN_TILE
    nq = seq // tq
    return pl.pallas_call(
        _attn_prompt_kernel,
        grid=(batch, nq),
        in_specs=[
            pl.BlockSpec((MLA_HEADS, tq, LANES), lambda b, i: (0, b * nq + i, 0)),
            pl.BlockSpec((MLA_HEADS, seq, LANES), lambda b, i: (0, b, 0)),
            pl.BlockSpec((MLA_HEADS, seq, LANES), lambda b, i: (0, b, 0)),
        ],
        out_specs=pl.BlockSpec((tq, MLA_HEADS * V_DIM), lambda b, i: (b * nq + i, 0)),
        out_shape=jax.ShapeDtypeStruct((batch * seq, MLA_HEADS * V_DIM), BF16),
        scratch_shapes=[pltpu.VMEM((MLA_HEADS, tq, LANES), F32),
                        pltpu.VMEM((MLA_HEADS, tq, LANES), F32)],
        compiler_params=pltpu.CompilerParams(dimension_semantics=("parallel", "arbitrary"),
                                             vmem_limit_bytes=VMEM_LIMIT),
        name="attn_prompt",
    )(q, k, v)


def _attn_sample_kernel(pt_ref, q_ref, knew_ref, ckv_hbm, kpe_hbm, o_ref, cbuf, pbuf, sems, *,
                        layer, n_pages):
    b = pl.program_id(0)
    nb = pl.num_programs(0)
    slot = b % 2

    def page_copies(bb, sl):
        copies = []
        for pg in range(n_pages):
            page = pt_ref[bb * n_pages + pg]
            rows = pl.ds(pg * PAGE_SIZE, PAGE_SIZE)
            copies.append(pltpu.make_async_copy(ckv_hbm.at[layer, page], cbuf.at[sl, rows],
                                                sems.at[0, sl]))
            copies.append(pltpu.make_async_copy(kpe_hbm.at[layer, page], pbuf.at[sl, :, rows],
                                                sems.at[1, sl]))
        return copies

    @pl.when(b == 0)
    def _():
        for cp in page_copies(b, slot):
            cp.start()

    @pl.when(b + 1 < nb)
    def _():
        for cp in page_copies(b + 1, 1 - slot):
            cp.start()

    for cp in page_copies(b, slot):
        cp.wait()

    q = q_ref[...]
    q_lat = q[:, :KV_LORA]
    o_pe = KV_LORA + QK_NOPE
    q_pe = q[:, o_pe:o_pe + QK_ROPE]
    knew = knew_ref[...]
    s_new = _dot_nt(q, knew) * ATTN_SCALE
    t_row = lax.broadcasted_iota(jnp.int32, s_new.shape, 0) // MLA_HEADS
    t_key = lax.broadcasted_iota(jnp.int32, s_new.shape, 1)
    s_new = jnp.where(t_key <= t_row, s_new, -jnp.inf)
    span = cbuf.shape[1] // SAMPLE_KEY_SPANS
    spans = [slice(c * span, (c + 1) * span) for c in range(SAMPLE_KEY_SPANS)]
    kcs = [cbuf[slot, sp, :].astype(BF16) for sp in spans]
    s_lat = [_dot_nt(q_lat, kc) for kc in kcs]
    s_pe = [_dot(q_pe, pbuf[slot, :, sp].astype(BF16)) for sp in spans]
    scores = [(a + b) * ATTN_SCALE for a, b in zip(s_lat, s_pe)]
    m = jnp.max(s_new, axis=-1, keepdims=True)
    for s in scores:
        m = jnp.maximum(m, jnp.max(s, axis=-1, keepdims=True))
    p_new = jnp.exp(s_new - m)
    denom = jnp.sum(p_new, axis=-1, keepdims=True)
    outs = [_dot(p_new.astype(BF16), knew[:, :KV_LORA])]
    for s, kc in zip(scores, kcs):
        p = jnp.exp(s - m)
        denom = denom + jnp.sum(p, axis=-1, keepdims=True)
        outs.append(_dot(p.astype(BF16), kc))
    while len(outs) > 1:
        outs = [outs[k] + outs[k + 1] if k + 1 < len(outs) else outs[k]
                for k in range(0, len(outs), 2)]
    o_ref[...] = (outs[0] / denom).astype(o_ref.dtype)


def _attn_sample(q_s, knew, cache_ckv, cache_kpe_t, page_table, j):
    nb, rows, _ = q_s.shape
    n_pages = page_table.shape[1]
    past = n_pages * PAGE_SIZE
    kernel = functools.partial(_attn_sample_kernel, layer=j, n_pages=n_pages)
    grid_spec = pltpu.PrefetchScalarGridSpec(
        num_scalar_prefetch=1,
        grid=(nb,),
        in_specs=[
            pl.BlockSpec((None, rows, QK_CAT), lambda b, pt: (b, 0, 0)),
            pl.BlockSpec((None, knew.shape[1], QK_CAT), lambda b, pt: (b, 0, 0)),
            pl.BlockSpec(memory_space=pl.ANY),
            pl.BlockSpec(memory_space=pl.ANY),
        ],
        out_specs=pl.BlockSpec((None, rows, KV_LORA), lambda b, pt: (b, 0, 0)),
        scratch_shapes=[
            pltpu.VMEM((2, past, KV_LORA), F32),
            pltpu.VMEM((2, QK_ROPE, past), F32),
            pltpu.SemaphoreType.DMA((2, 2)),
        ],
    )
    return pl.pallas_call(
        kernel,
        grid_spec=grid_spec,
        out_shape=jax.ShapeDtypeStruct((nb, rows, KV_LORA), BF16),
        compiler_params=pltpu.CompilerParams(dimension_semantics=("arbitrary",),
                                             vmem_limit_bytes=VMEM_LIMIT),
        name="attn_sample",
    )(page_table.reshape(-1), q_s, knew, cache_ckv, cache_kpe_t)


def _uv_kernel(o_ref, wuv_ref, a_ref):
    outs = [_dot(o_ref[:, hh * KV_LORA:(hh + 1) * KV_LORA], wuv_ref[hh])
            for hh in range(MLA_HEADS)]
    a_ref[...] = jnp.concatenate(outs, axis=-1).astype(a_ref.dtype)


def _uv_project(o_lat, prm, j):
    n = o_lat.shape[0]
    return pl.pallas_call(
        _uv_kernel,
        grid=(1,),
        in_specs=[_const_spec((n, MLA_HEADS * KV_LORA)),
                  _layer_spec((MLA_HEADS, KV_LORA, V_DIM), j)],
        out_specs=pl.BlockSpec((n, MLA_HEADS * V_DIM), lambda i: (0, 0)),
        out_shape=jax.ShapeDtypeStruct((n, MLA_HEADS * V_DIM), BF16),
        compiler_params=pltpu.CompilerParams(dimension_semantics=("arbitrary",),
                                             vmem_limit_bytes=VMEM_LIMIT),
        name="uv_project",
    )(o_lat, prm["w_uv"])


def _ffn_and_gate(h1, p_ref, gffn_ref, wg_ref, wu_ref, wd_ref, gpe_ref, wpg_ref, wpe_ref):
    hn = _rms(h1, gffn_ref[...]).astype(BF16)
    act = (jax.nn.silu(_dot(hn, wg_ref[...])) * _dot(hn, wu_ref[...])).astype(BF16)
    h2 = h1 + _dot(act, wd_ref[...])
    gate = jax.nn.sigmoid(_dot(_rms(h2, gpe_ref[...]).astype(BF16), wpg_ref[...]))
    return h2 + _dot(p_ref[...].astype(BF16), wpe_ref[...]) * gate


def _post_ab_kernel(h_ref, attn_ref, lru_ref, p_ref, wout_ref, gffn_ref, wg_ref, wu_ref, wd_ref,
                    gpe_ref, wpg_ref, wpe_ref, out_ref):
    n_attn = MLA_HEADS * V_DIM
    mix = _dot(attn_ref[...], wout_ref[:n_attn, :]) + _dot(lru_ref[...], wout_ref[n_attn:, :])
    h1 = h_ref[...] + mix
    out_ref[...] = _ffn_and_gate(h1, p_ref, gffn_ref, wg_ref, wu_ref, wd_ref, gpe_ref, wpg_ref,
                                 wpe_ref)


def _post_c_kernel(h_ref, p_ref, gmix_ref, winc_ref, lng_ref, lnb_ref, ws_ref, bs_ref, woutc_ref,
                   gffn_ref, wg_ref, wu_ref, wd_ref, gpe_ref, wpg_ref, wpe_ref, gfin_ref,
                   out_ref, *rest, block_len, final):
    sp_sc = rest[-1]
    h = h_ref[...]
    tm = h.shape[0]
    z = jax.nn.gelu(_dot(_rms(h, gmix_ref[...]).astype(BF16), winc_ref[...]))
    u = z[:, :GMLP_WIDTH]
    v = z[:, GMLP_WIDTH:]
    vc = v - jnp.mean(v, axis=-1, keepdims=True)
    v = vc * lax.rsqrt(jnp.mean(vc * vc, axis=-1, keepdims=True) + EPS) * lng_ref[...] + lnb_ref[...]
    if len(rest) == 2:
        rest[0][...] = v
    vb = v.astype(BF16)
    t_idx = lax.broadcasted_iota(jnp.int32, (CHUNK, CHUNK), 0)
    s_idx = lax.broadcasted_iota(jnp.int32, (CHUNK, CHUNK), 1)
    mask = (s_idx <= t_idx) & ((s_idx // block_len) == (t_idx // block_len))
    for gg in range(GMLP_GROUPS):
        wm = jnp.where(mask, ws_ref[gg], 0.0).astype(BF16)
        cols = slice(gg * GMLP_GROUP_DIM, (gg + 1) * GMLP_GROUP_DIM)
        for cc in range(tm // CHUNK):
            rows = slice(cc * CHUNK, (cc + 1) * CHUNK)
            sp_sc[rows, cols] = _dot(wm, vb[rows, cols]) + bs_ref[:, cols]
    mix = _dot((u * sp_sc[...]).astype(BF16), woutc_ref[...])
    h3 = _ffn_and_gate(h + mix, p_ref, gffn_ref, wg_ref, wu_ref, wd_ref, gpe_ref, wpg_ref, wpe_ref)
    out_ref[...] = _rms(h3, gfin_ref[...]) if final else h3


def _ffn_specs(i):
    return [
        _layer_spec((1, D_MODEL), i),
        _layer_spec((D_MODEL, FFN_HIDDEN), i),
        _layer_spec((D_MODEL, FFN_HIDDEN), i),
        _layer_spec((FFN_HIDDEN, D_MODEL), i),
        _layer_spec((1, D_MODEL), i),
        _layer_spec((D_MODEL, D_MODEL), i),
        _layer_spec((PLE_DIM, D_MODEL), i),
    ]


def _ffn_args(prm):
    return (prm["g_ffn"], prm["w_gate"], prm["w_up"], prm["w_down"], prm["g_pe"], prm["w_pg"],
            prm["w_pe"])


def _post_ab(h, attn, lru_y, p, prm, i):
    n = h.shape[0]
    tm = POST_AB_TILE
    j = i // 2
    row = lambda w: pl.BlockSpec((tm, w), lambda r: (r, 0))
    return pl.pallas_call(
        _post_ab_kernel,
        grid=(n // tm,),
        in_specs=[
            row(D_MODEL), row(MLA_HEADS * V_DIM), row(LRU_WIDTH),
            pl.BlockSpec((None, tm, PLE_DIM), lambda r: (i, r, 0)),
            _layer_spec((MLA_HEADS * V_DIM + LRU_WIDTH, D_MODEL), j),
        ] + _ffn_specs(i),
        out_specs=row(D_MODEL),
        out_shape=jax.ShapeDtypeStruct((n, D_MODEL), F32),
        compiler_params=pltpu.CompilerParams(dimension_semantics=("parallel",),
                                             vmem_limit_bytes=VMEM_LIMIT),
        name="post_ab",
    )(h, attn, lru_y, p, prm["w_out_ab"], *_ffn_args(prm))


def _post_c(h, p, prm, i, ws_tiled, bs_full, block_len, emit_v):
    n = h.shape[0]
    tm = POST_C_TILE
    j = i // 2
    final = i == DEPTH - 1
    row = lambda w: pl.BlockSpec((tm, w), lambda r: (r, 0))
    kernel = functools.partial(_post_c_kernel, block_len=block_len, final=final)
    out_specs = [row(D_MODEL)] + ([row(GMLP_WIDTH)] if emit_v else [])
    out_shape = [jax.ShapeDtypeStruct((n, D_MODEL), F32)]
    if emit_v:
        out_shape.append(jax.ShapeDtypeStruct((n, GMLP_WIDTH), F32))
    outs = pl.pallas_call(
        kernel,
        grid=(n // tm,),
        in_specs=[
            row(D_MODEL),
            pl.BlockSpec((None, tm, PLE_DIM), lambda r: (i, r, 0)),
            _layer_spec((1, D_MODEL), i),
            _layer_spec((D_MODEL, 2 * GMLP_WIDTH), j),
            _layer_spec((1, GMLP_WIDTH), j),
            _layer_spec((1, GMLP_WIDTH), j),
            _layer_spec((GMLP_GROUPS, CHUNK, CHUNK), j),
            _layer_spec((CHUNK, GMLP_WIDTH), j),
            _layer_spec((GMLP_WIDTH, D_MODEL), j),
        ] + _ffn_specs(i) + [_const_spec((1, D_MODEL))],
        out_specs=out_specs,
        out_shape=out_shape,
        scratch_shapes=[pltpu.VMEM((tm, GMLP_WIDTH), F32)],
        compiler_params=pltpu.CompilerParams(dimension_semantics=("parallel",),
                                             vmem_limit_bytes=VMEM_LIMIT),
        name="post_c",
    )(h, p, prm["g_mix"], prm["w_in_c"], prm["ln_g_c"], prm["ln_b_c"], ws_tiled, bs_full,
      prm["w_out_c"], *_ffn_args(prm), prm["g_final"])
    return outs[0], (outs[1] if emit_v else None)


def _rope_tables(pos, reps=1):
    inv = jnp.exp(-math.log(ROPE_THETA) * jnp.arange(ROPE_HALF, dtype=F32) / ROPE_HALF)
    ang = pos.astype(F32)[:, None] * inv[None, :]
    cos, sin = jnp.cos(ang), jnp.sin(ang)
    t = pos.shape[0]
    head = jnp.zeros((t, QK_NOPE), F32)
    tail = jnp.zeros((t, LANES - QK_NOPE - QK_ROPE), F32)
    csk = jnp.concatenate([head, cos, cos, tail], axis=1)
    snk = jnp.concatenate([head, -sin, sin, tail], axis=1)
    log2_scale = ATTN_SCALE * math.log2(math.e)
    csq = jnp.concatenate([head + 1.0, cos, cos, tail], axis=1) * log2_scale
    tables = {"csk": csk, "snk": snk, "csq": csq, "snq": snk * log2_scale}
    return {name: jnp.tile(tab, (reps, 1)) for name, tab in tables.items()}


def _prepare_params(g_mix, g_ffn, g_pe, g_final, w_in_ab, g_qnorm, g_kvnorm, w_uq, w_uk, w_uv,
                    conv_w, conv_b, w_rg, b_rg, w_ig, b_ig, lru_lambda, w_out_ab, w_in_c, ln_g_c,
                    ln_b_c, w_s, b_s, w_out_c, w_gate, w_up, w_down, w_pe, w_pg):
    n_ab = w_in_ab.shape[0]
    o1 = Q_LORA + KV_LORA
    o2 = o1 + QK_ROPE
    pad_tail = LANES - QK_NOPE - QK_ROPE - ROPE_HALF
    w_in = jnp.concatenate(
        [w_in_ab[:, :, :o1], w_in_ab[:, :, o2:], jnp.zeros((n_ab, D_MODEL, QK_NOPE), F32),
         w_in_ab[:, :, o1:o2], w_in_ab[:, :, o1:o1 + ROPE_HALF],
         jnp.zeros((n_ab, D_MODEL, pad_tail), F32)], axis=2)
    uq = w_uq.reshape(n_ab, Q_LORA, MLA_HEADS, QK_NOPE + QK_ROPE)
    uq = jnp.concatenate([uq, uq[..., QK_NOPE:QK_NOPE + ROPE_HALF],
                          jnp.zeros(uq.shape[:3] + (pad_tail,), F32)], axis=3)
    uq = uq.reshape(n_ab, Q_LORA, MLA_HEADS * LANES)
    uk_t = jnp.pad(w_uk.transpose(0, 3, 1, 2), ((0, 0), (0, 0), (0, 0), (0, LANES - QK_NOPE)))
    uk_t = uk_t.reshape(n_ab, KV_LORA, MLA_HEADS * LANES)
    uv_cat = jnp.pad(w_uv.transpose(0, 2, 1, 3), ((0, 0), (0, 0), (0, 0), (0, LANES - V_DIM)))
    uv_cat = uv_cat.reshape(n_ab, KV_LORA, MLA_HEADS * LANES)

    def block_diag(w):
        eye = jnp.eye(LRU_HEADS, dtype=F32)
        return jnp.einsum("jhab,hg->jhagb", w, eye).reshape(n_ab, LRU_WIDTH, LRU_WIDTH)

    vec = lambda a: a[:, None, :]
    return {
        "g_mix": vec(g_mix), "g_ffn": vec(g_ffn), "g_pe": vec(g_pe), "g_final": g_final[None, :],
        "w_in_ab": w_in.astype(BF16), "g_qnorm": vec(g_qnorm), "g_kvnorm": vec(g_kvnorm),
        "w_uq": uq.astype(BF16),
        "w_uk": w_uk.astype(BF16), "w_uv": w_uv.astype(BF16),
        "w_uk_t": uk_t.astype(BF16), "w_uv_cat": uv_cat.astype(BF16),
        "conv_w": conv_w, "conv_b": vec(conv_b),
        "w_gate_lru": jnp.concatenate([block_diag(w_rg), block_diag(w_ig)], axis=2).astype(BF16),
        "b_rg": vec(b_rg), "b_ig": vec(b_ig), "lru_lambda": vec(lru_lambda),
        "w_out_ab": w_out_ab.astype(BF16),
        "w_in_c": w_in_c.astype(BF16), "ln_g_c": vec(ln_g_c), "ln_b_c": vec(ln_b_c),
        "w_s": w_s, "b_s": b_s, "w_out_c": w_out_c.astype(BF16),
        "w_gate": w_gate.astype(BF16), "w_up": w_up.astype(BF16), "w_down": w_down.astype(BF16),
        "w_pe": w_pe.astype(BF16), "w_pg": w_pg.astype(BF16),
    }


def _gmlp_spatial(prm, block_len):
    reps = CHUNK // block_len
    ws = jnp.tile(prm["w_s"][:, :, :block_len, :block_len], (1, 1, reps, reps))
    bs = jnp.tile(prm["b_s"][:, :, :block_len], (1, 1, reps))
    bs = jnp.repeat(jnp.swapaxes(bs, 1, 2), GMLP_GROUP_DIM, axis=2)
    return ws, bs


def _trunk(x, p, tables, n_pos_tiles, prm, past, batch, seq):
    n = batch * seq
    h = x.reshape(n, D_MODEL)
    p = p.reshape(DEPTH, n, PLE_DIM)
    block_len = min(seq, CHUNK)
    ws_tiled, bs_full = _gmlp_spatial(prm, block_len)
    ckv, kpe, lru, conv, vrows = [], [], [], [], []
    for i in range(DEPTH):
        j = i // 2
        if i % 2 == 0:
            if past is None:
                q, k, v, c_kv, k_pe, zx, zg = _in_even(h, tables, prm, j, n_pos_tiles, False)
                attn = _attn_prompt(q, k, v, batch, seq)
                lru_y, h_last, buf = _lru_prompt(zx, zg, prm, j, batch, seq)
                h_last = h_last.reshape(batch, LRU_WIDTH)
            else:
                qcat, kcat, c_kv, k_pe, zx, zg = _in_even(h, tables, prm, j, n_pos_tiles, True)
                q_s = qcat.reshape(MLA_HEADS, batch, seq, QK_CAT).transpose(1, 2, 0, 3)
                q_s = q_s.reshape(batch, seq * MLA_HEADS, QK_CAT)
                knew = jnp.pad(kcat.reshape(batch, seq, QK_CAT),
                               ((0, 0), (0, 2 * SUBLANES - seq), (0, 0)))
                o_lat = _attn_sample(q_s, knew, past["cache_ckv"], past["cache_kpe_t"],
                                     past["page_table"], j)
                attn = _uv_project(o_lat.reshape(n, MLA_HEADS * KV_LORA), prm, j)
                to_t = lambda a: a.reshape(batch, seq, LRU_WIDTH).transpose(1, 0, 2)
                y_t, h_last = _lru_sample(to_t(zx), to_t(zg),
                                          past["state_conv"][j].transpose(1, 0, 2),
                                          past["state_lru"][j], prm, j)
                lru_y = y_t.transpose(1, 0, 2).reshape(n, LRU_WIDTH)
                buf = jnp.concatenate([past["state_conv"][j], zx.reshape(batch, seq, LRU_WIDTH)],
                                      axis=1)[:, seq:]
            h = _post_ab(h, attn, lru_y, p, prm, i)
            ckv.append(c_kv.reshape(batch, seq, KV_LORA))
            kpe.append(k_pe.reshape(batch, seq, QK_ROPE))
            lru.append(h_last)
            conv.append(buf)
        else:
            h, v = _post_c(h, p, prm, i, ws_tiled, bs_full, block_len, past is not None)
            if v is not None:
                vrows.append(v.reshape(batch, seq, GMLP_WIDTH))
    return h.reshape(batch, seq, D_MODEL), ckv, kpe, lru, conv, vrows


def kernel(x_prompt, x_sample, cache_ckv, cache_kpe, state_lru, state_conv, page_table, p_prompt, p_sample, g_mix, g_ffn, g_pe, g_final, w_in_ab, g_qnorm, g_kvnorm, w_uq, w_uk, w_uv, conv_w, conv_b, w_rg, b_rg, w_ig, b_ig, lru_lambda, w_out_ab, w_in_c, ln_g_c, ln_b_c, w_s, b_s, w_out_c, w_gate, w_up, w_down, w_pe, w_pg):
    prm = _prepare_params(g_mix, g_ffn, g_pe, g_final, w_in_ab, g_qnorm, g_kvnorm, w_uq, w_uk,
                          w_uv, conv_w, conv_b, w_rg, b_rg, w_ig, b_ig, lru_lambda, w_out_ab,
                          w_in_c, ln_g_c, ln_b_c, w_s, b_s, w_out_c, w_gate, w_up, w_down, w_pe,
                          w_pg)
    batch, seq, _ = x_prompt.shape
    dec_batch, dec_seq, _ = x_sample.shape
    past_len = page_table.shape[1] * PAGE_SIZE

    tables_p = _rope_tables(jnp.arange(seq, dtype=jnp.int32))
    y_prompt, ckv_p, kpe_p, lru_p, conv_p, _ = _trunk(
        x_prompt, p_prompt, tables_p, seq // IN_TILE, prm, None, batch, seq)

    tables_s = _rope_tables(past_len + jnp.arange(dec_seq, dtype=jnp.int32), IN_TILE // dec_seq)
    past = {"cache_ckv": cache_ckv, "cache_kpe_t": jnp.swapaxes(cache_kpe, 2, 3), "state_lru": state_lru,
            "state_conv": state_conv, "page_table": page_table}
    y_sample, ckv_s, kpe_s, lru_s, conv_s, v_s = _trunk(
        x_sample, p_sample, tables_s, 1, prm, past, dec_batch, dec_seq)

    return (y_prompt, y_sample,
            jnp.stack(ckv_p), jnp.stack(kpe_p), jnp.stack(lru_p), jnp.stack(conv_p),
            jnp.stack(ckv_s), jnp.stack(kpe_s), jnp.stack(lru_s), jnp.stack(conv_s),
            jnp.stack(v_s))
```
